```python
import math
import jax
import jax.numpy as jnp
from jax import lax
import numpy as np

D_MODEL = 1024
BATCH = 4
SEQ = 8192
DEPTH = 1

MIX_WIDTH = D_MODEL
RWKV_WIDTH = MIX_WIDTH // 2
POOL_WIDTH = MIX_WIDTH - RWKV_WIDTH
HEAD_DIM = 64
RWKV_HEADS = RWKV_WIDTH // HEAD_DIM
DECAY_LORA = 64
AAA_LORA = 64
GATE_LORA = 128
POOL_WINDOWS = (2, 4, 8, 16)
POOL_GROUPS = len(POOL_WINDOWS)
POOL_GROUP_DIM = POOL_WIDTH // POOL_GROUPS
RWKV_COLS = 3 * RWKV_WIDTH + DECAY_LORA + AAA_LORA + GATE_LORA
IN_COLS = RWKV_COLS + POOL_WIDTH
RWKV_SPLITS = (RWKV_WIDTH, 2 * RWKV_WIDTH, 3 * RWKV_WIDTH,
               3 * RWKV_WIDTH + DECAY_LORA, 3 * RWKV_WIDTH + DECAY_LORA + AAA_LORA)
D_FF = 2816
CONV_WIDTH = 3
NORM_EPS = 1e-6
GN_EPS = 64e-5

kernel_name = 'hymba_rwkv7_pool_convffn'


def rms_norm(x, gain):
    xf = x.astype(jnp.float32)
    y = xf * lax.rsqrt(jnp.mean(xf * xf, axis=-1, keepdims=True) + NORM_EPS)
    return (y * gain.astype(jnp.float32)).astype(x.dtype)


def token_shift(u):
    return jnp.pad(u, ((0, 0), (1, 0), (0, 0)))[:, :-1]


def rwkv7_recurrence(r, w, k, v, a, b):
    bsz, _, n_heads, n = r.shape

    def step(state, inp):
        r_t, w_t, k_t, v_t, a_t, b_t = inp
        sa = jnp.einsum('bhvk,bhk->bhv', state, a_t)
        state = (state * w_t[:, :, None, :] + sa[..., None] * b_t[:, :, None, :]
                 + v_t[..., None] * k_t[:, :, None, :])
        y_t = jnp.einsum('bhvk,bhk->bhv', state, r_t)
        return state, y_t

    seq = (jnp.moveaxis(r, 1, 0), jnp.moveaxis(w, 1, 0), jnp.moveaxis(k, 1, 0),
           jnp.moveaxis(v, 1, 0), jnp.moveaxis(a, 1, 0), jnp.moveaxis(b, 1, 0))
    s0 = jnp.zeros((bsz, n_heads, n, n), jnp.float32)
    _, y = lax.scan(step, s0, seq)
    return jnp.moveaxis(y, 0, 1)


def rwkv7_group(u, mu_shift, w0, w_decay_up, a0, w_aaa_up, w_gate_up, k_k, k_a, r_k, ln_x_w, ln_x_b):
    bsz, t_len, _ = u.shape
    f = lambda p: p.astype(jnp.float32)
    u = u + f(mu_shift) * (token_shift(u) - u)
    r, k, v, wl, al, gl = jnp.split(u, RWKV_SPLITS, axis=-1)
    w_log = -jax.nn.softplus(-(f(w0) + jnp.tanh(wl) @ f(w_decay_up))) - 0.5
    decay = jnp.exp(-jnp.exp(w_log))
    alpha = jax.nn.sigmoid(f(a0) + al @ f(w_aaa_up))
    g = jax.nn.sigmoid(gl) @ f(w_gate_up)
    hs = lambda z: z.reshape(bsz, t_len, RWKV_HEADS, HEAD_DIM)
    kk = hs(k * f(k_k))
    kk = kk * lax.rsqrt(jnp.maximum(jnp.sum(kk * kk, axis=-1, keepdims=True), 1e-24))
    alpha_h = hs(alpha)
    k_mod = hs(k * (1.0 + (alpha - 1.0) * f(k_a)))
    r_h, v_h = hs(r), hs(v)
    y = rwkv7_recurrence(r_h, hs(decay), k_mod, v_h, -kk, kk * alpha_h)
    mean = jnp.mean(y, axis=-1, keepdims=True)
    var = jnp.mean(jnp.square(y - mean), axis=-1, keepdims=True)
    y = ((y - mean) * lax.rsqrt(var + GN_EPS) * f(ln_x_w).reshape(RWKV_HEADS, HEAD_DIM)
         + f(ln_x_b).reshape(RWKV_HEADS, HEAD_DIM))
    bonus = jnp.sum(r_h * k_mod * f(r_k), axis=-1, keepdims=True) * v_h
    return (y + bonus).reshape(bsz, t_len, RWKV_WIDTH) * g


def multiscale_pool(u, w_pool, pool_scale):
    bsz, t_len, _ = u.shape
    ug = u.reshape(bsz, t_len, POOL_GROUPS, POOL_GROUP_DIM)
    cs = lax.cumsum(ug, axis=1)
    pos = jnp.arange(t_len)
    outs = []
    for gi, win in enumerate(POOL_WINDOWS):
        c = cs[:, :, gi]
        lag = jnp.pad(c, ((0, 0), (win, 0), (0, 0)))[:, :t_len]
        cnt = jnp.minimum(pos + 1, win).astype(jnp.float32)[None, :, None]
        outs.append((c - lag) / cnt - ug[:, :, gi])
    pooled = jnp.stack(outs, axis=2)
    mixed = jnp.einsum('btgc,gcd->btgd', pooled, w_pool.astype(jnp.float32))
    return mixed.reshape(bsz, t_len, POOL_WIDTH) * pool_scale.astype(jnp.float32)


def conv_ffn(h, w_up, conv_w, conv_b, w_down):
    up = h @ w_up
    up = lax.conv_general_dilated(up, conv_w[:, None, :], window_strides=(1,),
                                  padding=((CONV_WIDTH - 1, 0),),
                                  dimension_numbers=('NWC', 'WIO', 'NWC'),
                                  feature_group_count=2 * D_FF) + conv_b
    gate, val = jnp.split(up, 2, axis=-1)
    return (jax.nn.gelu(gate, approximate=True) * val) @ w_down


def setup_inputs(seed: int = 0) -> dict:
    key = jax.random.key(seed)
    ks = jax.random.split(key, 24)
    n = lambda k, s: jax.random.normal(k, s, jnp.float32)
    L = DEPTH
    return {
        'x': n(ks[0], (BATCH, SEQ, D_MODEL)),
        'pre_mix_norm': 1.0 + 0.05 * n(ks[1], (L, D_MODEL)),
        'w_in': n(ks[2], (L, D_MODEL, IN_COLS)) * D_MODEL ** -0.5,
        'mu_shift': jax.random.uniform(ks[3], (L, RWKV_COLS), jnp.float32),
        'w0': -1.0 + 0.5 * n(ks[4], (L, RWKV_WIDTH)),
        'w_decay_up': n(ks[5], (L, DECAY_LORA, RWKV_WIDTH)) * 0.5 * DECAY_LORA ** -0.5,
        'a0': 0.1 * n(ks[6], (L, RWKV_WIDTH)),
        'w_aaa_up': n(ks[7], (L, AAA_LORA, RWKV_WIDTH)) * 0.5 * AAA_LORA ** -0.5,
        'w_gate_up': n(ks[8], (L, GATE_LORA, RWKV_WIDTH)) * GATE_LORA ** -0.5,
        'k_k': 0.85 + 0.05 * n(ks[9], (L, RWKV_WIDTH)),
        'k_a': 1.0 + 0.05 * n(ks[10], (L, RWKV_WIDTH)),
        'r_k': 0.1 * n(ks[11], (L, RWKV_HEADS, HEAD_DIM)),
        'ln_x_w': 1.0 + 0.05 * n(ks[12], (L, RWKV_WIDTH)),
        'ln_x_b': 0.01 * n(ks[13], (L, RWKV_WIDTH)),
        'w_pool': n(ks[14], (L, POOL_GROUPS, POOL_GROUP_DIM, POOL_GROUP_DIM)) * POOL_GROUP_DIM ** -0.5,
        'pool_scale': 1.0 + 0.05 * n(ks[15], (L, POOL_WIDTH)),
        'w_out': n(ks[16], (L, MIX_WIDTH, D_MODEL)) * MIX_WIDTH ** -0.5,
        'post_mix_norm': 1.0 + 0.05 * n(ks[17], (L, D_MODEL)),
        'pre_ffn_norm': 1.0 + 0.05 * n(ks[18], (L, D_MODEL)),
        'w_up': n(ks[19], (L, D_MODEL, 2 * D_FF)) * D_MODEL ** -0.5,
        'conv_w': n(ks[20], (L, CONV_WIDTH, 2 * D_FF)) * CONV_WIDTH ** -0.5,
        'conv_b': 0.01 * n(ks[21], (L, 2 * D_FF)),
        'w_down': n(ks[22], (L, D_FF, D_MODEL)) * D_FF ** -0.5,
        'post_ffn_norm': 1.0 + 0.05 * n(ks[23], (L, D_MODEL)),
    }


def reference(x, pre_mix_norm, w_in, mu_shift, w0, w_decay_up, a0, w_aaa_up, w_gate_up,
              k_k, k_a, r_k, ln_x_w, ln_x_b, w_pool, pool_scale, w_out, post_mix_norm,
              pre_ffn_norm, w_up, conv_w, conv_b, w_down, post_ffn_norm):
    for layer in range(DEPTH):
        h = rms_norm(x, pre_mix_norm[layer])
        proj = (h @ w_in[layer]).astype(jnp.float32)
        y_rwkv = rwkv7_group(proj[..., :RWKV_COLS], mu_shift[layer], w0[layer], w_decay_up[layer],
                             a0[layer], w_aaa_up[layer], w_gate_up[layer], k_k[layer], k_a[layer],
                             r_k[layer], ln_x_w[layer], ln_x_b[layer])
        y_pool = multiscale_pool(proj[..., RWKV_COLS:], w_pool[layer], pool_scale[layer])
        mixed = jnp.concatenate([y_rwkv, y_pool], axis=-1).astype(x.dtype) @ w_out[layer]
        x = x + rms_norm(mixed, post_mix_norm[layer])
        h = rms_norm(x, pre_ffn_norm[layer])
        f = conv_ffn(h, w_up[layer], conv_w[layer], conv_b[layer], w_down[layer])
        x = x + rms_norm(f, post_ffn_norm[layer])
    return x
```

```python
import functools
import math

import jax
import jax.numpy as jnp
from jax import lax
from jax.experimental import pallas as pl
from jax.experimental.pallas import tpu as pltpu

F32 = jnp.float32
BF16 = jnp.bfloat16

HEAD_DIM = 64
POOL_WINDOWS = (2, 4, 8, 16)
DECAY_LORA = 64
AAA_LORA = 64
GATE_LORA = 128
CONV_WIDTH = 3
NORM_EPS = 1e-6
GN_EPS = 64e-5
EXP_NEG_HALF = math.exp(-0.5)

CHUNK = 64
GROUP_HEADS = 4
GROUP_LANES = GROUP_HEADS * HEAD_DIM
POOL_HISTORY = 16
CONV_HISTORY = 8
VMEM_LIMIT_BYTES = 56 * 1024 * 1024

HIGHEST = lax.Precision.HIGHEST


def _sigmoid(z):
    return 1.0 / (1.0 + jnp.exp(-z))


def _rms_norm(x, gain):
    ms = jnp.mean(x * x, axis=-1, keepdims=True)
    return x * lax.rsqrt(ms + NORM_EPS) * gain


def _dot(a, b, precision=None):
    return jnp.dot(a, b, preferred_element_type=F32, precision=precision)


def _dot_nt(a, b, precision=None):
    return lax.dot_general(a, b, (((1,), (1,)), ((), ())), preferred_element_type=F32, precision=precision)


def _dot_tn(a, b, precision=None):
    return lax.dot_general(a, b, (((0,), (0,)), ((), ())), preferred_element_type=F32, precision=precision)


def _split_dot(a, ones_bf16):
    hi = a.astype(BF16)
    lo = (a - hi.astype(F32)).astype(BF16)
    return _dot(hi, ones_bf16) + _dot(lo, ones_bf16)


def _mix_in_kernel(x_ref, g1_ref, w_in_ref, mu_ref, w0_ref, wd_ref, a0_ref, wa_ref, wg_ref, kk_ref, ka_ref,
                   wpool_ref, pscale_ref, ones_ref,
                   r_out, k_out, v_out, lw_out, a_out, b_out, g_out, pool_out,
                   ucarry_ref, pcarry_ref, *, tile, rwkv_width, pool_group_dim):
    t = pl.program_id(1)
    rwkv_cols = 3 * rwkv_width + DECAY_LORA + AAA_LORA + GATE_LORA

    @pl.when(t == 0)
    def _():
        ucarry_ref[...] = jnp.zeros_like(ucarry_ref)
        pcarry_ref[...] = jnp.zeros_like(pcarry_ref)

    x = x_ref[0]
    h = _rms_norm(x, g1_ref[...]).astype(BF16)
    proj = _dot(h, w_in_ref[...])

    row = lax.broadcasted_iota(jnp.int32, (tile, 1), 0)

    u = proj[:, :rwkv_cols]
    prev = pltpu.roll(u, 1, 0)
    prev = jnp.where(row == 0, ucarry_ref[...], prev)
    ucarry_ref[...] = u[tile - 1:tile, :]
    u = u + mu_ref[...] * (prev - u)

    w = rwkv_width
    r = u[:, 0:w]
    k = u[:, w:2 * w]
    v = u[:, 2 * w:3 * w]
    o = 3 * w
    wl = u[:, o:o + DECAY_LORA]
    al = u[:, o + DECAY_LORA:o + DECAY_LORA + AAA_LORA]
    gl = u[:, o + DECAY_LORA + AAA_LORA:rwkv_cols]

    z = w0_ref[...] + _dot(jnp.tanh(wl).astype(BF16), wd_ref[...])
    lw = -EXP_NEG_HALF * _sigmoid(z)
    alpha = _sigmoid(a0_ref[...] + _dot(al.astype(BF16), wa_ref[...]))
    g = _dot(_sigmoid(gl).astype(BF16), wg_ref[...])

    kk = k * kk_ref[...]
    ss = _split_dot(kk * kk, ones_ref[...])
    kk = kk * lax.rsqrt(jnp.maximum(ss, 1e-24))
    k_mod = k * (1.0 + (alpha - 1.0) * ka_ref[...])

    r_out[0] = r
    k_out[0] = k_mod
    v_out[0] = v
    lw_out[0] = lw
    a_out[0] = -kk
    b_out[0] = kk * alpha
    g_out[0] = g

    up = proj[:, rwkv_cols:]
    ext = jnp.concatenate([pcarry_ref[...], up], axis=0)
    pcarry_ref[...] = up[tile - POOL_HISTORY:, :]
    pos = (t * tile + row + 1).astype(F32)
    gd = pool_group_dim
    for gi, win in enumerate(POOL_WINDOWS):
        s = ext[:, gi * gd:(gi + 1) * gd]
        step = 1
        while step < win:
            s = s + pltpu.roll(s, step, 0)
            step *= 2
        cnt = jnp.minimum(pos, float(win))
        pooled = s[POOL_HISTORY:, :] / cnt - up[:, gi * gd:(gi + 1) * gd]
        mixed = _dot(pooled.astype(BF16), wpool_ref[gi])
        pool_out[0, :, gi * gd:(gi + 1) * gd] = mixed * pscale_ref[:, gi * gd:(gi + 1) * gd]


def _mix_in(x, g1, w_in, mu, w0, wd, a0, wa, wg, k_k, k_a, w_pool, pool_scale, ones, *, tile):
    bsz, t_len, d_model = x.shape
    rwkv_width = w0.shape[-1]
    pool_width = pool_scale.shape[-1]
    pool_group_dim = w_pool.shape[-1]
    rwkv_cols = mu.shape[-1]
    grid = (bsz, t_len // tile)

    def full(a):
        nd = a.ndim
        return pl.BlockSpec(a.shape, lambda b, t, _nd=nd: (0,) * _nd)

    seq_spec = lambda width: pl.BlockSpec((1, tile, width), lambda b, t: (b, t, 0))
    out_sds = lambda width: jax.ShapeDtypeStruct((bsz, t_len, width), F32)
    params = (g1, w_in, mu, w0, wd, a0, wa, wg, k_k, k_a, w_pool, pool_scale, ones)
    kern = functools.partial(_mix_in_kernel, tile=tile, rwkv_width=rwkv_width, pool_group_dim=pool_group_dim)
    return pl.pallas_call(
        kern,
        grid=grid,
        in_specs=[seq_spec(d_model)] + [full(p) for p in params],
        out_specs=[seq_spec(rwkv_width)] * 7 + [seq_spec(pool_width)],
        out_shape=[out_sds(rwkv_width)] * 7 + [out_sds(pool_width)],
        scratch_shapes=[pltpu.VMEM((1, rwkv_cols), F32), pltpu.VMEM((POOL_HISTORY, pool_width), F32)],
        compiler_params=pltpu.CompilerParams(dimension_semantics=("arbitrary", "arbitrary"),
                                             vmem_limit_bytes=VMEM_LIMIT_BYTES),
        name="mix_in",
    )(x, *params)


def _block_diag(x, block_mask):
    return jnp.where(block_mask, jnp.concatenate([x] * GROUP_HEADS, axis=0), 0.0)


def _rwkv_kernel(r_ref, k_ref, v_ref, lw_ref, a_ref, b_ref, g_ref, rk_ref, lnw_ref, lnb_ref, y_ref, s_ref,
                 *, chunks):
    t = pl.program_id(2)

    @pl.when(t == 0)
    def _():
        s_ref[...] = jnp.zeros_like(s_ref)

    c_len, gl = CHUNK, GROUP_LANES
    row = lax.broadcasted_iota(jnp.int32, (c_len, gl), 0)
    lane_pos = lax.broadcasted_iota(jnp.int32, (c_len, gl), 1) % c_len
    strict = lane_pos < row
    incl = lane_pos <= row
    eye = (lane_pos == row).astype(F32)
    bm_r = lax.broadcasted_iota(jnp.int32, (gl, gl), 0) // HEAD_DIM
    bm_c = lax.broadcasted_iota(jnp.int32, (gl, gl), 1) // HEAD_DIM
    block_mask = bm_r == bm_c
    head_ones = block_mask.astype(F32)
    tri = (lax.broadcasted_iota(jnp.int32, (c_len, c_len), 1)
           <= lax.broadcasted_iota(jnp.int32, (c_len, c_len), 0)).astype(F32)
    rk = rk_ref[...]
    lnw = lnw_ref[...]
    lnb = lnb_ref[...]
    bd = lambda m: _block_diag(m, block_mask)

    def chunk_body(c, carry):
        sl = pl.ds(pl.multiple_of(c * c_len, c_len), c_len)
        r = r_ref[0, sl, :]
        k = k_ref[0, sl, :]
        v = v_ref[0, sl, :]
        lw = lw_ref[0, sl, :]
        a = a_ref[0, sl, :]
        b = b_ref[0, sl, :]
        g = g_ref[0, sl, :]

        cl = _dot(tri, lw, HIGHEST)
        cl_last = cl[c_len - 1:c_len, :]
        at = a * jnp.exp(cl - lw)
        rt = r * jnp.exp(cl)
        e_neg = jnp.exp(-cl)
        bt = b * e_neg
        kt = k * e_neg
        e_end = jnp.exp(cl_last - cl)
        bh = b * e_end
        kh = k * e_end
        p_end = jnp.exp(cl_last)

        ar = jnp.concatenate([at, rt], axis=0)
        ab = _dot_nt(ar, bd(bt), HIGHEST)
        ak = _dot_nt(ar, bd(kt), HIGHEST)
        a_ab = jnp.where(strict, ab[:c_len], 0.0)
        a_rb = jnp.where(incl, ab[c_len:], 0.0)
        a_ak = jnp.where(strict, ak[:c_len], 0.0)
        a_rk = jnp.where(incl, ak[c_len:], 0.0)

        lp = _dot(a_ab, bd(a_ab), HIGHEST)
        tm = eye + a_ab
        n_sq = int(math.log2(c_len)) - 1
        for j in range(1, n_sq + 1):
            wmat = bd(lp)
            if j < n_sq:
                res = _dot(jnp.concatenate([lp, tm], axis=0), wmat, HIGHEST)
                lp = res[:c_len]
                tm = tm + res[c_len:]
            else:
                tm = tm + _dot(tm, wmat, HIGHEST)

        s_bd = s_ref[...]
        sar = _dot_nt(ar, s_bd, HIGHEST)
        av = _dot(jnp.concatenate([a_ak, a_rk], axis=0), bd(v), HIGHEST)
        rhs = sar[:c_len] + av[:c_len]
        u = _dot(tm, bd(rhs), HIGHEST)
        y = sar[c_len:] + av[c_len:] + _dot(a_rb, bd(u), HIGHEST)

        upd = _dot_tn(jnp.concatenate([u, v], axis=0), jnp.concatenate([bh, kh], axis=0), HIGHEST)
        s_ref[...] = s_bd * p_end + jnp.where(block_mask, upd, 0.0)

        mean = _dot(y, head_ones, HIGHEST) * (1.0 / HEAD_DIM)
        d = y - mean
        var = _dot(d * d, head_ones, HIGHEST) * (1.0 / HEAD_DIM)
        yn = d * lax.rsqrt(var + GN_EPS) * lnw + lnb
        bonus = _dot(r * k * rk, head_ones, HIGHEST) * v
        y_ref[0, sl, :] = (yn + bonus) * g
        return carry

    lax.fori_loop(0, chunks, chunk_body, 0)


def _rwkv(r, k, v, lw, a, b, g, r_k, ln_w, ln_b, *, tile):
    bsz, t_len, width = r.shape
    groups = width // GROUP_LANES
    chunks = tile // CHUNK
    grid = (bsz, groups, t_len // tile)
    seq_spec = pl.BlockSpec((1, tile, GROUP_LANES), lambda bi, gi, ti: (bi, ti, gi))
    par_spec = pl.BlockSpec((1, GROUP_LANES), lambda bi, gi, ti: (0, gi))
    return pl.pallas_call(
        functools.partial(_rwkv_kernel, chunks=chunks),
        grid=grid,
        in_specs=[seq_spec] * 7 + [par_spec] * 3,
        out_specs=seq_spec,
        out_shape=jax.ShapeDtypeStruct((bsz, t_len, width), F32),
        scratch_shapes=[pltpu.VMEM((GROUP_LANES, GROUP_LANES), F32)],
        compiler_params=pltpu.CompilerParams(dimension_semantics=("arbitrary", "arbitrary", "arbitrary"),
                                             vmem_limit_bytes=VMEM_LIMIT_BYTES),
        name="rwkv",
    )(r, k, v, lw, a, b, g, r_k, ln_w, ln_b)


def _gelu_tanh(x):
    c = math.sqrt(2.0 / math.pi)
    return x * (0.5 * (1.0 + jnp.tanh(c * (x + 0.044715 * (x * x * x)))))


def _mix_out_kernel(x_ref, yr_ref, yp_ref, w_out_ref, g2_ref, g3_ref, w_up_ref, cw_ref, cb_ref, w_down_ref, g4_ref,
                    out_ref, ucarry_ref, act_ref, *, tile, d_ff, col_chunk):
    t = pl.program_id(1)

    @pl.when(t == 0)
    def _():
        ucarry_ref[...] = jnp.zeros_like(ucarry_ref)

    ycat = jnp.concatenate([yr_ref[0], yp_ref[0]], axis=-1).astype(BF16)
    mixed = _dot(ycat, w_out_ref[...])
    x1 = x_ref[0] + _rms_norm(mixed, g2_ref[...])
    h2 = _rms_norm(x1, g3_ref[...]).astype(BF16)

    def conv_cols(c0):
        cols = slice(c0, c0 + col_chunk)
        up = _dot(h2, w_up_ref[:, cols])
        ext = jnp.concatenate([ucarry_ref[:, cols], up], axis=0)
        ucarry_ref[:, cols] = up[tile - CONV_HISTORY:, :]
        s1 = pltpu.roll(ext, 1, 0)[CONV_HISTORY:, :]
        s2 = pltpu.roll(ext, 2, 0)[CONV_HISTORY:, :]
        cw = cw_ref[:, cols]
        return up * cw[2:3, :] + s1 * cw[1:2, :] + s2 * cw[0:1, :] + cb_ref[:, cols]

    for c in range(d_ff // col_chunk):
        c0 = c * col_chunk
        gate = conv_cols(c0)
        val = conv_cols(d_ff + c0)
        act_ref[:, c0:c0 + col_chunk] = (_gelu_tanh(gate) * val).astype(BF16)

    f = _dot(act_ref[...], w_down_ref[...])
    out_ref[0] = x1 + _rms_norm(f, g4_ref[...])


def _mix_out(x, y_rwkv, y_pool, w_out, g2, g3, w_up, conv_w, conv_b, w_down, g4, *, tile, col_chunk):
    bsz, t_len, d_model = x.shape
    d_ff = w_down.shape[0]
    grid = (bsz, t_len // tile)

    def full(a):
        nd = a.ndim
        return pl.BlockSpec(a.shape, lambda b, t, _nd=nd: (0,) * _nd)

    seq_spec = lambda width: pl.BlockSpec((1, tile, width), lambda b, t: (b, t, 0))
    params = (w_out, g2, g3, w_up, conv_w, conv_b, w_down, g4)
    kern = functools.partial(_mix_out_kernel, tile=tile, d_ff=d_ff, col_chunk=col_chunk)
    return pl.pallas_call(
        kern,
        grid=grid,
        in_specs=[seq_spec(d_model), seq_spec(y_rwkv.shape[-1]), seq_spec(y_pool.shape[-1])]
                 + [full(p) for p in params],
        out_specs=seq_spec(d_model),
        out_shape=jax.ShapeDtypeStruct((bsz, t_len, d_model), F32),
        scratch_shapes=[pltpu.VMEM((CONV_HISTORY, 2 * d_ff), F32), pltpu.VMEM((tile, d_ff), BF16)],
        compiler_params=pltpu.CompilerParams(dimension_semantics=("arbitrary", "arbitrary"),
                                             vmem_limit_bytes=VMEM_LIMIT_BYTES),
        name="mix_out",
    )(x, y_rwkv, y_pool, *params)


def _layer(x, pre_mix_norm, w_in, mu_shift, w0, w_decay_up, a0, w_aaa_up, w_gate_up, k_k, k_a, r_k, ln_x_w, ln_x_b,
           w_pool, pool_scale, w_out, post_mix_norm, pre_ffn_norm, w_up, conv_w, conv_b, w_down, post_ffn_norm,
           *, in_tile, rwkv_tile, out_tile, col_chunk):
    row = lambda p: p.reshape(1, -1).astype(F32)
    rwkv_width = w0.shape[-1]
    head_id = jnp.arange(rwkv_width) // HEAD_DIM
    ones = (head_id[:, None] == head_id[None, :]).astype(BF16)

    r, k, v, lw, a, b, g, y_pool = _mix_in(
        x, row(pre_mix_norm), w_in.astype(BF16), row(mu_shift), row(w0), w_decay_up.astype(BF16), row(a0),
        w_aaa_up.astype(BF16), w_gate_up.astype(BF16), row(k_k), row(k_a), w_pool.astype(BF16), row(pool_scale),
        ones, tile=in_tile)
    y_rwkv = _rwkv(r, k, v, lw, a, b, g, row(r_k), row(ln_x_w), row(ln_x_b), tile=rwkv_tile)
    return _mix_out(x, y_rwkv, y_pool, w_out.astype(BF16), row(post_mix_norm), row(pre_ffn_norm),
                    w_up.astype(BF16), conv_w.astype(F32), row(conv_b), w_down.astype(BF16), row(post_ffn_norm),
                    tile=out_tile, col_chunk=col_chunk)


def kernel(x, pre_mix_norm, w_in, mu_shift, w0, w_decay_up, a0, w_aaa_up, w_gate_up, k_k, k_a, r_k, ln_x_w, ln_x_b,
           w_pool, pool_scale, w_out, post_mix_norm, pre_ffn_norm, w_up, conv_w, conv_b, w_down, post_ffn_norm):
    depth = w_in.shape[0]
    for layer in range(depth):
        x = _layer(x, pre_mix_norm[layer], w_in[layer], mu_shift[layer], w0[layer], w_decay_up[layer], a0[layer],
                   w_aaa_up[layer], w_gate_up[layer], k_k[layer], k_a[layer], r_k[layer], ln_x_w[layer],
                   ln_x_b[layer], w_pool[layer], pool_scale[layer], w_out[layer], post_mix_norm[layer],
                   pre_ffn_norm[layer], w_up[layer], conv_w[layer], conv_b[layer], w_down[layer],
                   post_ffn_norm[layer], in_tile=256, rwkv_tile=512, out_tile=256, col_chunk=256)
    return x
```

```python
import functools
import math

import jax
import jax.numpy as jnp
from jax import lax
from jax.experimental import pallas as pl
from jax.experimental.pallas import tpu as pltpu

F32 = jnp.float32
BF16 = jnp.bfloat16

HEAD_DIM = 64
POOL_WINDOWS = (2, 4, 8, 16)
DECAY_LORA = 64
AAA_LORA = 64
GATE_LORA = 128
CONV_WIDTH = 3
NORM_EPS = 1e-6
GN_EPS = 64e-5
EXP_NEG_HALF = math.exp(-0.5)

CHUNK = 64
GROUP_HEADS = 4
GROUP_LANES = GROUP_HEADS * HEAD_DIM
POOL_HISTORY = 16
CONV_HISTORY = 8
VMEM_LIMIT_BYTES = 56 * 1024 * 1024


def _sigmoid(z):
    return 1.0 / (1.0 + jnp.exp(-z))


def _rms_norm(x, gain):
    ms = jnp.mean(x * x, axis=-1, keepdims=True)
    return x * lax.rsqrt(ms + NORM_EPS) * gain


def _dot(a, b, precision=None):
    return jnp.dot(a, b, preferred_element_type=F32, precision=precision)


def _dot_nt(a, b, precision=None):
    return lax.dot_general(a, b, (((1,), (1,)), ((), ())), preferred_element_type=F32, precision=precision)


def _dot_tn(a, b, precision=None):
    return lax.dot_general(a, b, (((0,), (0,)), ((), ())), preferred_element_type=F32, precision=precision)


def _split_dot(a, ones_bf16):
    hi = a.astype(BF16)
    lo = (a - hi.astype(F32)).astype(BF16)
    return _dot(hi, ones_bf16) + _dot(lo, ones_bf16)


def _mix_in_kernel(x_ref, g1_ref, w_in_ref, mu_ref, w0_ref, wd_ref, a0_ref, wa_ref, wg_ref, kk_ref, ka_ref,
                   wpool_ref, pscale_ref, ones_ref,
                   r_out, k_out, v_out, lw_out, a_out, b_out, g_out, pool_out,
                   ucarry_ref, pcarry_ref, *, tile, rwkv_width, pool_group_dim):
    t = pl.program_id(1)
    rwkv_cols = 3 * rwkv_width + DECAY_LORA + AAA_LORA + GATE_LORA

    @pl.when(t == 0)
    def _():
        ucarry_ref[...] = jnp.zeros_like(ucarry_ref)
        pcarry_ref[...] = jnp.zeros_like(pcarry_ref)

    x = x_ref[0]
    h = _rms_norm(x, g1_ref[...]).astype(BF16)
    proj = _dot(h, w_in_ref[...])

    row = lax.broadcasted_iota(jnp.int32, (tile, 1), 0)

    u = proj[:, :rwkv_cols]
    prev = pltpu.roll(u, 1, 0)
    prev = jnp.where(row == 0, ucarry_ref[...], prev)
    ucarry_ref[...] = u[tile - 1:tile, :]
    u = u + mu_ref[...] * (prev - u)

    w = rwkv_width
    r = u[:, 0:w]
    k = u[:, w:2 * w]
    v = u[:, 2 * w:3 * w]
    o = 3 * w
    wl = u[:, o:o + DECAY_LORA]
    al = u[:, o + DECAY_LORA:o + DECAY_LORA + AAA_LORA]
    gl = u[:, o + DECAY_LORA + AAA_LORA:rwkv_cols]

    z = w0_ref[...] + _dot(jnp.tanh(wl).astype(BF16), wd_ref[...])
    lw = -EXP_NEG_HALF * _sigmoid(z)
    alpha = _sigmoid(a0_ref[...] + _dot(al.astype(BF16), wa_ref[...]))
    g = _dot(_sigmoid(gl).astype(BF16), wg_ref[...])

    kk = k * kk_ref[...]
    ss = _split_dot(kk * kk, ones_ref[...])
    kk = kk * lax.rsqrt(jnp.maximum(ss, 1e-24))
    k_mod = k * (1.0 + (alpha - 1.0) * ka_ref[...])

    r_out[0] = r
    k_out[0] = k_mod
    v_out[0] = v
    lw_out[0] = lw
    a_out[0] = -kk
    b_out[0] = kk * alpha
    g_out[0] = g

    up = proj[:, rwkv_cols:]
    ext = jnp.concatenate([pcarry_ref[...], up], axis=0)
    pcarry_ref[...] = up[tile - POOL_HISTORY:, :]
    pos = (t * tile + row + 1).astype(F32)
    gd = pool_group_dim
    for gi, win in enumerate(POOL_WINDOWS):
        s = ext[:, gi * gd:(gi + 1) * gd]
        step = 1
        while step < win:
            s = s + pltpu.roll(s, step, 0)
            step *= 2
        cnt = jnp.minimum(pos, float(win))
        pooled = s[POOL_HISTORY:, :] / cnt - up[:, gi * gd:(gi + 1) * gd]
        mixed = _dot(pooled.astype(BF16), wpool_ref[gi])
        pool_out[0, :, gi * gd:(gi + 1) * gd] = mixed * pscale_ref[:, gi * gd:(gi + 1) * gd]


def _mix_in(x, g1, w_in, mu, w0, wd, a0, wa, wg, k_k, k_a, w_pool, pool_scale, ones, *, tile):
    bsz, t_len, d_model = x.shape
    rwkv_width = w0.shape[-1]
    pool_width = pool_scale.shape[-1]
    pool_group_dim = w_pool.shape[-1]
    rwkv_cols = mu.shape[-1]
    grid = (bsz, t_len // tile)

    def full(a):
        nd = a.ndim
        return pl.BlockSpec(a.shape, lambda b, t, _nd=nd: (0,) * _nd)

    seq_spec = lambda width: pl.BlockSpec((1, tile, width), lambda b, t: (b, t, 0))
    out_sds = lambda width: jax.ShapeDtypeStruct((bsz, t_len, width), F32)
    params = (g1, w_in, mu, w0, wd, a0, wa, wg, k_k, k_a, w_pool, pool_scale, ones)
    kern = functools.partial(_mix_in_kernel, tile=tile, rwkv_width=rwkv_width, pool_group_dim=pool_group_dim)
    return pl.pallas_call(
        kern,
        grid=grid,
        in_specs=[seq_spec(d_model)] + [full(p) for p in params],
        out_specs=[seq_spec(rwkv_width)] * 7 + [seq_spec(pool_width)],
        out_shape=[out_sds(rwkv_width)] * 7 + [out_sds(pool_width)],
        scratch_shapes=[pltpu.VMEM((1, rwkv_cols), F32), pltpu.VMEM((POOL_HISTORY, pool_width), F32)],
        compiler_params=pltpu.CompilerParams(dimension_semantics=("arbitrary", "arbitrary"),
                                             vmem_limit_bytes=VMEM_LIMIT_BYTES),
        name="mix_in",
    )(x, *params)


def _block_diag(x, block_mask):
    return jnp.where(block_mask, jnp.concatenate([x] * GROUP_HEADS, axis=0), jnp.zeros((), x.dtype))


def _split3_dot(m_bf16, a):
    hi = a.astype(BF16)
    r1 = a - hi.astype(F32)
    mid = r1.astype(BF16)
    lo = (r1 - mid.astype(F32)).astype(BF16)
    return _dot(m_bf16, hi) + _dot(m_bf16, mid) + _dot(m_bf16, lo)


def _rwkv_kernel(r_ref, k_ref, v_ref, lw_ref, a_ref, b_ref, g_ref, rk_ref, lnw_ref, lnb_ref, y_ref, s_ref,
                 *, batch_block, groups, chunks):
    t = pl.program_id(1)

    @pl.when(t == 0)
    def _():
        s_ref[...] = jnp.zeros_like(s_ref)

    c_len, gl = CHUNK, GROUP_LANES
    row = lax.broadcasted_iota(jnp.int32, (c_len, gl), 0)
    lane_pos = lax.broadcasted_iota(jnp.int32, (c_len, gl), 1) % c_len
    strict = lane_pos < row
    incl = lane_pos <= row
    eye = (lane_pos == row).astype(F32)
    bm_r = lax.broadcasted_iota(jnp.int32, (gl, gl), 0) // HEAD_DIM
    bm_c = lax.broadcasted_iota(jnp.int32, (gl, gl), 1) // HEAD_DIM
    block_mask = bm_r == bm_c
    head_ones = block_mask.astype(BF16)
    tri = (lax.broadcasted_iota(jnp.int32, (c_len, c_len), 1)
           <= lax.broadcasted_iota(jnp.int32, (c_len, c_len), 0)).astype(BF16)
    bd = lambda m: _block_diag(m.astype(BF16), block_mask)
    bf = lambda m: m.astype(BF16)

    def chunk_stages(bi, gi, c):
        rows = slice(c * c_len, (c + 1) * c_len)
        lanes = slice(gi * gl, (gi + 1) * gl)
        si = bi * groups + gi
        lw = lw_ref[bi, rows, lanes]
        cl = _split3_dot(tri, lw)
        yield

        cl_last = cl[c_len - 1:c_len, :]
        a = a_ref[bi, rows, lanes]
        b = b_ref[bi, rows, lanes]
        r = r_ref[bi, rows, lanes]
        k = k_ref[bi, rows, lanes]
        e_neg = jnp.exp(-cl)
        e_end = jnp.exp(cl_last - cl)
        bh = b * e_end
        kh = k * e_end
        p_end = jnp.exp(cl_last)
        ar = bf(jnp.concatenate([a * jnp.exp(cl - lw), r * jnp.exp(cl)], axis=0))
        ab = _dot_nt(ar, bd(b * e_neg))
        ak = _dot_nt(ar, bd(k * e_neg))
        yield

        a_ab = jnp.where(strict, ab[:c_len], 0.0)
        a_rb = jnp.where(incl, ab[c_len:], 0.0)
        a_ak = jnp.where(strict, ak[:c_len], 0.0)
        a_rk = jnp.where(incl, ak[c_len:], 0.0)
        lp = _dot(bf(a_ab), bd(a_ab))
        tm = eye + a_ab
        yield
        n_sq = int(math.log2(c_len)) - 1
        for j in range(1, n_sq + 1):
            wmat = bd(lp)
            if j < n_sq:
                res = _dot(bf(jnp.concatenate([lp, tm], axis=0)), wmat)
                lp = res[:c_len]
                tm = tm + res[c_len:]
            else:
                tm = tm + _dot(bf(tm), wmat)
            yield

        v = v_ref[bi, rows, lanes]
        s_bd = s_ref[si]
        sar = _dot_nt(ar, bf(s_bd))
        av = _dot(bf(jnp.concatenate([a_ak, a_rk], axis=0)), bd(v))
        yield

        u = _dot(bf(tm), bd(sar[:c_len] + av[:c_len]))
        yield

        y = sar[c_len:] + av[c_len:] + _dot(bf(a_rb), bd(u))
        upd = _dot_tn(bf(jnp.concatenate([u, v], axis=0)), bf(jnp.concatenate([bh, kh], axis=0)))
        yield

        s_ref[si] = s_bd * p_end + jnp.where(block_mask, upd, 0.0)
        mean = _split_dot(y, head_ones) * (1.0 / HEAD_DIM)
        bonus = _split_dot(r * k * rk_ref[:, lanes], head_ones) * v
        yield

        d = y - mean
        var = _split_dot(d * d, head_ones) * (1.0 / HEAD_DIM)
        yield

        yn = d * lax.rsqrt(var + GN_EPS) * lnw_ref[:, lanes] + lnb_ref[:, lanes]
        y_ref[bi, rows, lanes] = (yn + bonus) * g_ref[bi, rows, lanes]

    for c in range(chunks):
        streams = [chunk_stages(bi, gi, c) for bi in range(batch_block) for gi in range(groups)]
        while streams:
            alive = []
            for s in streams:
                try:
                    next(s)
                    alive.append(s)
                except StopIteration:
                    pass
            streams = alive


def _rwkv(r, k, v, lw, a, b, g, r_k, ln_w, ln_b, *, tile, batch_block):
    bsz, t_len, width = r.shape
    groups = width // GROUP_LANES
    chunks = tile // CHUNK
    grid = (bsz // batch_block, t_len // tile)
    seq_spec = pl.BlockSpec((batch_block, tile, width), lambda bi, ti: (bi, ti, 0))
    par_spec = pl.BlockSpec((1, width), lambda bi, ti: (0, 0))
    return pl.pallas_call(
        functools.partial(_rwkv_kernel, batch_block=batch_block, groups=groups, chunks=chunks),
        grid=grid,
        in_specs=[seq_spec] * 7 + [par_spec] * 3,
        out_specs=seq_spec,
        out_shape=jax.ShapeDtypeStruct((bsz, t_len, width), F32),
        scratch_shapes=[pltpu.VMEM((batch_block * groups, GROUP_LANES, GROUP_LANES), F32)],
        compiler_params=pltpu.CompilerParams(dimension_semantics=("arbitrary", "arbitrary"),
                                             vmem_limit_bytes=VMEM_LIMIT_BYTES),
        name="rwkv",
    )(r, k, v, lw, a, b, g, r_k, ln_w, ln_b)


def _gelu_tanh(x):
    c = math.sqrt(2.0 / math.pi)
    return x * (0.5 * (1.0 + jnp.tanh(c * (x + 0.044715 * (x * x * x)))))


def _mix_out_kernel(x_ref, yr_ref, yp_ref, w_out_ref, g2_ref, g3_ref, w_up_ref, cw_ref, cb_ref, w_down_ref, g4_ref,
                    out_ref, ucarry_ref, act_ref, *, tile, d_ff, col_chunk):
    t = pl.program_id(1)

    @pl.when(t == 0)
    def _():
        ucarry_ref[...] = jnp.zeros_like(ucarry_ref)

    ycat = jnp.concatenate([yr_ref[0], yp_ref[0]], axis=-1).astype(BF16)
    mixed = _dot(ycat, w_out_ref[...])
    x1 = x_ref[0] + _rms_norm(mixed, g2_ref[...])
    h2 = _rms_norm(x1, g3_ref[...]).astype(BF16)

    def conv_cols(c0):
        cols = slice(c0, c0 + col_chunk)
        up = _dot(h2, w_up_ref[:, cols])
        ext = jnp.concatenate([ucarry_ref[:, cols], up], axis=0)
        ucarry_ref[:, cols] = up[tile - CONV_HISTORY:, :]
        s1 = pltpu.roll(ext, 1, 0)[CONV_HISTORY:, :]
        s2 = pltpu.roll(ext, 2, 0)[CONV_HISTORY:, :]
        cw = cw_ref[:, cols]
        return up * cw[2:3, :] + s1 * cw[1:2, :] + s2 * cw[0:1, :] + cb_ref[:, cols]

    for c in range(d_ff // col_chunk):
        c0 = c * col_chunk
        gate = conv_cols(c0)
        val = conv_cols(d_ff + c0)
        act_ref[:, c0:c0 + col_chunk] = (_gelu_tanh(gate) * val).astype(BF16)

    f = _dot(act_ref[...], w_down_ref[...])
    out_ref[0] = x1 + _rms_norm(f, g4_ref[...])


def _mix_out(x, y_rwkv, y_pool, w_out, g2, g3, w_up, conv_w, conv_b, w_down, g4, *, tile, col_chunk):
    bsz, t_len, d_model = x.shape
    d_ff = w_down.shape[0]
    grid = (bsz, t_len // tile)

    def full(a):
        nd = a.ndim
        return pl.BlockSpec(a.shape, lambda b, t, _nd=nd: (0,) * _nd)

    seq_spec = lambda width: pl.BlockSpec((1, tile, width), lambda b, t: (b, t, 0))
    params = (w_out, g2, g3, w_up, conv_w, conv_b, w_down, g4)
    kern = functools.partial(_mix_out_kernel, tile=tile, d_ff=d_ff, col_chunk=col_chunk)
    return pl.pallas_call(
        kern,
        grid=grid,
        in_specs=[seq_spec(d_model), seq_spec(y_rwkv.shape[-1]), seq_spec(y_pool.shape[-1])]
                 + [full(p) for p in params],
        out_specs=seq_spec(d_model),
        out_shape=jax.ShapeDtypeStruct((bsz, t_len, d_model), F32),
        scratch_shapes=[pltpu.VMEM((CONV_HISTORY, 2 * d_ff), F32), pltpu.VMEM((tile, d_ff), BF16)],
        compiler_params=pltpu.CompilerParams(dimension_semantics=("arbitrary", "arbitrary"),
                                             vmem_limit_bytes=VMEM_LIMIT_BYTES),
        name="mix_out",
    )(x, y_rwkv, y_pool, *params)


def _layer(x, pre_mix_norm, w_in, mu_shift, w0, w_decay_up, a0, w_aaa_up, w_gate_up, k_k, k_a, r_k, ln_x_w, ln_x_b,
           w_pool, pool_scale, w_out, post_mix_norm, pre_ffn_norm, w_up, conv_w, conv_b, w_down, post_ffn_norm,
           *, in_tile, rwkv_tile, rwkv_batch_block, out_tile, col_chunk):
    row = lambda p: p.reshape(1, -1).astype(F32)
    rwkv_width = w0.shape[-1]
    head_id = jnp.arange(rwkv_width) // HEAD_DIM
    ones = (head_id[:, None] == head_id[None, :]).astype(BF16)

    r, k, v, lw, a, b, g, y_pool = _mix_in(
        x, row(pre_mix_norm), w_in.astype(BF16), row(mu_shift), row(w0), w_decay_up.astype(BF16), row(a0),
        w_aaa_up.astype(BF16), w_gate_up.astype(BF16), row(k_k), row(k_a), w_pool.astype(BF16), row(pool_scale),
        ones, tile=in_tile)
    y_rwkv = _rwkv(r, k, v, lw, a, b, g, row(r_k), row(ln_x_w), row(ln_x_b), tile=rwkv_tile,
                   batch_block=rwkv_batch_block)
    return _mix_out(x, y_rwkv, y_pool, w_out.astype(BF16), row(post_mix_norm), row(pre_ffn_norm),
                    w_up.astype(BF16), conv_w.astype(F32), row(conv_b), w_down.astype(BF16), row(post_ffn_norm),
                    tile=out_tile, col_chunk=col_chunk)


def kernel(x, pre_mix_norm, w_in, mu_shift, w0, w_decay_up, a0, w_aaa_up, w_gate_up, k_k, k_a, r_k, ln_x_w, ln_x_b,
           w_pool, pool_scale, w_out, post_mix_norm, pre_ffn_norm, w_up, conv_w, conv_b, w_down, post_ffn_norm):
    depth = w_in.shape[0]
    for layer in range(depth):
        x = _layer(x, pre_mix_norm[layer], w_in[layer], mu_shift[layer], w0[layer], w_decay_up[layer], a0[layer],
                   w_aaa_up[layer], w_gate_up[layer], k_k[layer], k_a[layer], r_k[layer], ln_x_w[layer],
                   ln_x_b[layer], w_pool[layer], pool_scale[layer], w_out[layer], post_mix_norm[layer],
                   pre_ffn_norm[layer], w_up[layer], conv_w[layer], conv_b[layer], w_down[layer],
                   post_ffn_norm[layer], in_tile=256, rwkv_tile=128, rwkv_batch_block=4,
                   out_tile=256, col_chunk=256)
    return x
```

```python
import functools
import math

import jax
import jax.numpy as jnp
from jax import lax
from jax.experimental import pallas as pl
from jax.experimental.pallas import tpu as pltpu

F32 = jnp.float32
BF16 = jnp.bfloat16

HEAD_DIM = 64
POOL_WINDOWS = (2, 4, 8, 16)
DECAY_LORA = 64
AAA_LORA = 64
GATE_LORA = 128
CONV_WIDTH = 3
NORM_EPS = 1e-6
GN_EPS = 64e-5
EXP_NEG_HALF = math.exp(-0.5)

CHUNK = 64
GROUP_HEADS = 4
GROUP_LANES = GROUP_HEADS * HEAD_DIM
POOL_HISTORY = 16
CONV_HISTORY = 8
VMEM_LIMIT_BYTES = 56 * 1024 * 1024


def _sigmoid(z):
    return 1.0 / (1.0 + jnp.exp(-z))


def _rms_norm(x, gain):
    ms = jnp.mean(x * x, axis=-1, keepdims=True)
    return x * lax.rsqrt(ms + NORM_EPS) * gain


def _dot(a, b, precision=None):
    return jnp.dot(a, b, preferred_element_type=F32, precision=precision)


def _dot_nt(a, b, precision=None):
    return lax.dot_general(a, b, (((1,), (1,)), ((), ())), preferred_element_type=F32, precision=precision)


def _dot_tn(a, b, precision=None):
    return lax.dot_general(a, b, (((0,), (0,)), ((), ())), preferred_element_type=F32, precision=precision)


def _split_dot(a, ones_bf16):
    hi = a.astype(BF16)
    lo = (a - hi.astype(F32)).astype(BF16)
    return _dot(hi, ones_bf16) + _dot(lo, ones_bf16)


def _mix_in_kernel(x_ref, g1_ref, w_in_ref, mu_ref, w0_ref, wd_ref, a0_ref, wa_ref, wg_ref, kk_ref, ka_ref,
                   wpool_ref, pscale_ref, ones_ref,
                   r_out, k_out, v_out, lw_out, a_out, b_out, g_out, pool_out,
                   ucarry_ref, pcarry_ref, *, tile, rwkv_width, pool_group_dim):
    t = pl.program_id(1)
    rwkv_cols = 3 * rwkv_width + DECAY_LORA + AAA_LORA + GATE_LORA

    @pl.when(t == 0)
    def _():
        ucarry_ref[...] = jnp.zeros_like(ucarry_ref)
        pcarry_ref[...] = jnp.zeros_like(pcarry_ref)

    x = x_ref[0]
    h = _rms_norm(x, g1_ref[...]).astype(BF16)
    proj = _dot(h, w_in_ref[...])

    row = lax.broadcasted_iota(jnp.int32, (tile, 1), 0)

    u = proj[:, :rwkv_cols]
    prev = pltpu.roll(u, 1, 0)
    prev = jnp.where(row == 0, ucarry_ref[...], prev)
    ucarry_ref[...] = u[tile - 1:tile, :]
    u = u + mu_ref[...] * (prev - u)

    w = rwkv_width
    r = u[:, 0:w]
    k = u[:, w:2 * w]
    v = u[:, 2 * w:3 * w]
    o = 3 * w
    wl = u[:, o:o + DECAY_LORA]
    al = u[:, o + DECAY_LORA:o + DECAY_LORA + AAA_LORA]
    gl = u[:, o + DECAY_LORA + AAA_LORA:rwkv_cols]

    z = w0_ref[...] + _dot(jnp.tanh(wl).astype(BF16), wd_ref[...])
    lw = -EXP_NEG_HALF * _sigmoid(z)
    alpha = _sigmoid(a0_ref[...] + _dot(al.astype(BF16), wa_ref[...]))
    g = _dot(_sigmoid(gl).astype(BF16), wg_ref[...])

    kk = k * kk_ref[...]
    kk2 = (kk * kk).astype(BF16)
    gl = GROUP_LANES
    ss = jnp.concatenate([_dot(kk2[:, i * gl:(i + 1) * gl], ones_ref[...]) for i in range(w // gl)], axis=1)
    kk = kk * lax.rsqrt(jnp.maximum(ss, 1e-24))
    k_mod = k * (1.0 + (alpha - 1.0) * ka_ref[...])

    r_out[0] = r
    k_out[0] = k_mod
    v_out[0] = v
    lw_out[0] = lw
    a_out[0] = -kk
    b_out[0] = kk * alpha
    g_out[0] = g

    up = proj[:, rwkv_cols:]
    ext = jnp.concatenate([pcarry_ref[...], up], axis=0)
    pcarry_ref[...] = up[tile - POOL_HISTORY:, :]
    pos = (t * tile + row + 1).astype(F32)
    gd = pool_group_dim
    pooled = []
    for gi, win in enumerate(POOL_WINDOWS):
        s = ext[:, gi * gd:(gi + 1) * gd]
        step = 1
        while step < win:
            s = s + pltpu.roll(s, step, 0)
            step *= 2
        cnt = jnp.minimum(pos, float(win))
        pooled.append((s[POOL_HISTORY:, :] / cnt - up[:, gi * gd:(gi + 1) * gd]).astype(BF16))
    for j in range(len(POOL_WINDOWS) // 2):
        mixed = _dot(jnp.concatenate(pooled[2 * j:2 * j + 2], axis=1), wpool_ref[j])
        pool_out[0, :, 2 * j * gd:2 * (j + 1) * gd] = mixed * pscale_ref[:, 2 * j * gd:2 * (j + 1) * gd]


def _mix_in(x, g1, w_in, mu, w0, wd, a0, wa, wg, k_k, k_a, w_pool, pool_scale, ones, *, tile):
    bsz, t_len, d_model = x.shape
    rwkv_width = w0.shape[-1]
    pool_width = pool_scale.shape[-1]
    pool_group_dim = w_pool.shape[-1] // 2
    rwkv_cols = mu.shape[-1]
    grid = (bsz, t_len // tile)

    def full(a):
        nd = a.ndim
        return pl.BlockSpec(a.shape, lambda b, t, _nd=nd: (0,) * _nd)

    seq_spec = lambda width: pl.BlockSpec((1, tile, width), lambda b, t: (b, t, 0))
    out_sds = lambda width: jax.ShapeDtypeStruct((bsz, t_len, width), F32)
    params = (g1, w_in, mu, w0, wd, a0, wa, wg, k_k, k_a, w_pool, pool_scale, ones)
    kern = functools.partial(_mix_in_kernel, tile=tile, rwkv_width=rwkv_width, pool_group_dim=pool_group_dim)
    return pl.pallas_call(
        kern,
        grid=grid,
        in_specs=[seq_spec(d_model)] + [full(p) for p in params],
        out_specs=[seq_spec(rwkv_width)] * 7 + [seq_spec(pool_width)],
        out_shape=[out_sds(rwkv_width)] * 7 + [out_sds(pool_width)],
        scratch_shapes=[pltpu.VMEM((1, rwkv_cols), F32), pltpu.VMEM((POOL_HISTORY, pool_width), F32)],
        compiler_params=pltpu.CompilerParams(dimension_semantics=("arbitrary", "arbitrary"),
                                             vmem_limit_bytes=VMEM_LIMIT_BYTES),
        name="mix_in",
    )(x, *params)


def _block_diag(x, block_mask):
    return jnp.where(block_mask, jnp.concatenate([x] * GROUP_HEADS, axis=0), jnp.zeros((), x.dtype))


def _split3_dot(m3_bf16, a):
    hi = a.astype(BF16)
    r1 = a - hi.astype(F32)
    mid = r1.astype(BF16)
    lo = (r1 - mid.astype(F32)).astype(BF16)
    return _dot(m3_bf16, jnp.concatenate([hi, mid, lo], axis=0))


def _rwkv_kernel(r_ref, k_ref, v_ref, lw_ref, a_ref, b_ref, g_ref, rk_ref, lnw_ref, lnb_ref, y_ref, s_ref,
                 *, batch_block, groups, chunks):
    t = pl.program_id(1)

    @pl.when(t == 0)
    def _():
        s_ref[...] = jnp.zeros_like(s_ref)

    c_len, gl = CHUNK, GROUP_LANES
    row = lax.broadcasted_iota(jnp.int32, (c_len, gl), 0)
    lane_pos = lax.broadcasted_iota(jnp.int32, (c_len, gl), 1) % c_len
    strict = lane_pos < row
    incl = lane_pos <= row
    eye = (lane_pos == row).astype(F32)
    bm_r = lax.broadcasted_iota(jnp.int32, (gl, gl), 0) // HEAD_DIM
    bm_c = lax.broadcasted_iota(jnp.int32, (gl, gl), 1) // HEAD_DIM
    block_mask = bm_r == bm_c
    head_ones = block_mask.astype(BF16)
    tri = (lax.broadcasted_iota(jnp.int32, (c_len, c_len), 1)
           <= lax.broadcasted_iota(jnp.int32, (c_len, c_len), 0)).astype(BF16)
    tri3 = jnp.concatenate([tri, tri, tri], axis=1)
    bd = lambda m: _block_diag(m.astype(BF16), block_mask)
    bf = lambda m: m.astype(BF16)

    def chunk_stages(bi, gi, c):
        rows = slice(c * c_len, (c + 1) * c_len)
        lanes = slice(gi * gl, (gi + 1) * gl)
        si = bi * groups + gi
        lw = lw_ref[bi, rows, lanes]
        cl = _split3_dot(tri3, lw)
        yield

        cl_last = cl[c_len - 1:c_len, :]
        a = a_ref[bi, rows, lanes]
        b = b_ref[bi, rows, lanes]
        r = r_ref[bi, rows, lanes]
        k = k_ref[bi, rows, lanes]
        e_neg = jnp.exp(-cl)
        e_end = jnp.exp(cl_last - cl)
        bh = b * e_end
        kh = k * e_end
        p_end = jnp.exp(cl_last)
        ar = bf(jnp.concatenate([a * jnp.exp(cl - lw), r * jnp.exp(cl)], axis=0))
        ab = _dot_nt(ar, bd(b * e_neg))
        ak = _dot_nt(ar, bd(k * e_neg))
        yield

        a_ab = jnp.where(strict, ab[:c_len], 0.0)
        a_rb = jnp.where(incl, ab[c_len:], 0.0)
        a_ak = jnp.where(strict, ak[:c_len], 0.0)
        a_rk = jnp.where(incl, ak[c_len:], 0.0)
        lp = _dot(bf(a_ab), bd(a_ab))
        tm = eye + a_ab
        yield
        n_sq = int(math.log2(c_len)) - 1
        for j in range(1, n_sq + 1):
            wmat = bd(lp)
            if j < n_sq:
                res = _dot(bf(jnp.concatenate([lp, tm], axis=0)), wmat)
                lp = res[:c_len]
                tm = tm + res[c_len:]
            else:
                tm = tm + _dot(bf(tm), wmat)
            yield

        v = v_ref[bi, rows, lanes]
        s_bd = s_ref[si]
        sar = _dot_nt(ar, bf(s_bd))
        av = _dot(bf(jnp.concatenate([a_ak, a_rk], axis=0)), bd(v))
        yield

        u = _dot(bf(tm), bd(sar[:c_len] + av[:c_len]))
        yield

        y = sar[c_len:] + av[c_len:] + _dot(bf(a_rb), bd(u))
        upd = _dot_tn(bf(jnp.concatenate([u, v], axis=0)), bf(jnp.concatenate([bh, kh], axis=0)))
        yield

        s_ref[si] = s_bd * p_end + jnp.where(block_mask, upd, 0.0)
        sums = _dot(bf(jnp.concatenate([y, r * k * rk_ref[:, lanes]], axis=0)), head_ones)
        mean = sums[:c_len] * (1.0 / HEAD_DIM)
        bonus = sums[c_len:] * v
        yield

        d = y - mean
        var = _dot(bf(d * d), head_ones) * (1.0 / HEAD_DIM)
        yield

        yn = d * lax.rsqrt(var + GN_EPS) * lnw_ref[:, lanes] + lnb_ref[:, lanes]
        y_ref[bi, rows, lanes] = (yn + bonus) * g_ref[bi, rows, lanes]

    for c in range(chunks):
        streams = [chunk_stages(bi, gi, c) for bi in range(batch_block) for gi in range(groups)]
        while streams:
            alive = []
            for s in streams:
                try:
                    next(s)
                    alive.append(s)
                except StopIteration:
                    pass
            streams = alive


def _rwkv(r, k, v, lw, a, b, g, r_k, ln_w, ln_b, *, tile, batch_block):
    bsz, t_len, width = r.shape
    groups = width // GROUP_LANES
    chunks = tile // CHUNK
    grid = (bsz // batch_block, t_len // tile)
    seq_spec = pl.BlockSpec((batch_block, tile, width), lambda bi, ti: (bi, ti, 0))
    par_spec = pl.BlockSpec((1, width), lambda bi, ti: (0, 0))
    return pl.pallas_call(
        functools.partial(_rwkv_kernel, batch_block=batch_block, groups=groups, chunks=chunks),
        grid=grid,
        in_specs=[seq_spec] * 7 + [par_spec] * 3,
        out_specs=seq_spec,
        out_shape=jax.ShapeDtypeStruct((bsz, t_len, width), F32),
        scratch_shapes=[pltpu.VMEM((batch_block * groups, GROUP_LANES, GROUP_LANES), F32)],
        compiler_params=pltpu.CompilerParams(dimension_semantics=("arbitrary", "arbitrary"),
                                             vmem_limit_bytes=VMEM_LIMIT_BYTES),
        name="rwkv",
    )(r, k, v, lw, a, b, g, r_k, ln_w, ln_b)


def _gelu_tanh(x):
    c1 = -2.0 * math.sqrt(2.0 / math.pi) * math.log2(math.e)
    c2 = c1 * 0.044715
    return x / (1.0 + jnp.exp2(x * (c1 + c2 * (x * x))))


def _mix_out_kernel(x_ref, yr_ref, yp_ref, w_out_ref, g2_ref, g3_ref, w_up_ref, cw_ref, cb_ref, w_down_ref, g4_ref,
                    out_ref, ucarry_ref, act_ref, *, tile, d_ff, col_chunk):
    t = pl.program_id(1)

    @pl.when(t == 0)
    def _():
        ucarry_ref[...] = jnp.zeros_like(ucarry_ref)

    ycat = jnp.concatenate([yr_ref[0], yp_ref[0]], axis=-1).astype(BF16)
    mixed = _dot(ycat, w_out_ref[...])
    x1 = x_ref[0] + _rms_norm(mixed, g2_ref[...])
    h2 = _rms_norm(x1, g3_ref[...]).astype(BF16)

    def conv_cols(c0):
        cols = slice(c0, c0 + col_chunk)
        up = _dot(h2, w_up_ref[:, cols])
        ext = jnp.concatenate([ucarry_ref[:, cols], up], axis=0)
        ucarry_ref[:, cols] = up[tile - CONV_HISTORY:, :]
        s1 = pltpu.roll(ext, 1, 0)[CONV_HISTORY:, :]
        s2 = pltpu.roll(ext, 2, 0)[CONV_HISTORY:, :]
        return (up * cw_ref[2:3, cols] + s1 * cw_ref[1:2, cols] + s2 * cw_ref[0:1, cols]) + cb_ref[:, cols]

    for c in range(d_ff // col_chunk):
        c0 = c * col_chunk
        act = _gelu_tanh(conv_cols(c0)) * conv_cols(d_ff + c0)
        act_ref[:, c0:c0 + col_chunk] = act.astype(BF16)

    f = _dot(act_ref[...], w_down_ref[...])
    out_ref[0] = x1 + _rms_norm(f, g4_ref[...])


def _mix_out(x, y_rwkv, y_pool, w_out, g2, g3, w_up, conv_w, conv_b, w_down, g4, *, tile, col_chunk):
    bsz, t_len, d_model = x.shape
    d_ff = w_down.shape[0]
    grid = (bsz, t_len // tile)

    def full(a):
        nd = a.ndim
        return pl.BlockSpec(a.shape, lambda b, t, _nd=nd: (0,) * _nd, pipeline_mode=pl.Buffered(1))

    seq_spec = lambda width: pl.BlockSpec((1, tile, width), lambda b, t: (b, t, 0))
    params = (w_out, g2, g3, w_up, conv_w, conv_b, w_down, g4)
    kern = functools.partial(_mix_out_kernel, tile=tile, d_ff=d_ff, col_chunk=col_chunk)
    return pl.pallas_call(
        kern,
        grid=grid,
        in_specs=[seq_spec(d_model), seq_spec(y_rwkv.shape[-1]), seq_spec(y_pool.shape[-1])]
                 + [full(p) for p in params],
        out_specs=seq_spec(d_model),
        out_shape=jax.ShapeDtypeStruct((bsz, t_len, d_model), F32),
        scratch_shapes=[pltpu.VMEM((CONV_HISTORY, 2 * d_ff), F32), pltpu.VMEM((tile, d_ff), BF16)],
        compiler_params=pltpu.CompilerParams(dimension_semantics=("arbitrary", "arbitrary"),
                                             vmem_limit_bytes=VMEM_LIMIT_BYTES),
        name="mix_out",
    )(x, y_rwkv, y_pool, *params)


def _layer(x, pre_mix_norm, w_in, mu_shift, w0, w_decay_up, a0, w_aaa_up, w_gate_up, k_k, k_a, r_k, ln_x_w, ln_x_b,
           w_pool, pool_scale, w_out, post_mix_norm, pre_ffn_norm, w_up, conv_w, conv_b, w_down, post_ffn_norm,
           *, in_tile, rwkv_tile, rwkv_batch_block, out_tile, col_chunk):
    row = lambda p: p.reshape(1, -1).astype(F32)
    head_id = jnp.arange(GROUP_LANES) // HEAD_DIM
    ones = (head_id[:, None] == head_id[None, :]).astype(BF16)
    n_groups, gd, _ = w_pool.shape
    wp = w_pool.astype(BF16).reshape(n_groups // 2, 2, gd, gd)
    zero = jnp.zeros_like(wp[:, 0])
    w_pool_pairs = jnp.concatenate([jnp.concatenate([wp[:, 0], zero], axis=2),
                                    jnp.concatenate([zero, wp[:, 1]], axis=2)], axis=1)

    r, k, v, lw, a, b, g, y_pool = _mix_in(
        x, row(pre_mix_norm), w_in.astype(BF16), row(mu_shift), row(w0), w_decay_up.astype(BF16), row(a0),
        w_aaa_up.astype(BF16), w_gate_up.astype(BF16), row(k_k), row(k_a), w_pool_pairs, row(pool_scale),
        ones, tile=in_tile)
    y_rwkv = _rwkv(r, k, v, lw, a, b, g, row(r_k), row(ln_x_w), row(ln_x_b), tile=rwkv_tile,
                   batch_block=rwkv_batch_block)
    return _mix_out(x, y_rwkv, y_pool, w_out.astype(BF16), row(post_mix_norm), row(pre_ffn_norm),
                    w_up.astype(BF16), conv_w.astype(F32), row(conv_b), w_down.astype(BF16), row(post_ffn_norm),
                    tile=out_tile, col_chunk=col_chunk)


def kernel(x, pre_mix_norm, w_in, mu_shift, w0, w_decay_up, a0, w_aaa_up, w_gate_up, k_k, k_a, r_k, ln_x_w, ln_x_b,
           w_pool, pool_scale, w_out, post_mix_norm, pre_ffn_norm, w_up, conv_w, conv_b, w_down, post_ffn_norm):
    depth = w_in.shape[0]
    for layer in range(depth):
        x = _layer(x, pre_mix_norm[layer], w_in[layer], mu_shift[layer], w0[layer], w_decay_up[layer], a0[layer],
                   w_aaa_up[layer], w_gate_up[layer], k_k[layer], k_a[layer], r_k[layer], ln_x_w[layer],
                   ln_x_b[layer], w_pool[layer], pool_scale[layer], w_out[layer], post_mix_norm[layer],
                   pre_ffn_norm[layer], w_up[layer], conv_w[layer], conv_b[layer], w_down[layer],
                   post_ffn_norm[layer], in_tile=256, rwkv_tile=128, rwkv_batch_block=4,
                   out_tile=512, col_chunk=256)
    return x
```

```python
import functools
import math

import jax
import jax.numpy as jnp
from jax import lax
from jax.experimental import pallas as pl
from jax.experimental.pallas import tpu as pltpu

F32 = jnp.float32
BF16 = jnp.bfloat16

HEAD_DIM = 64
POOL_WINDOWS = (2, 4, 8, 16)
DECAY_LORA = 64
AAA_LORA = 64
GATE_LORA = 128
CONV_WIDTH = 3
NORM_EPS = 1e-6
GN_EPS = 64e-5
EXP_NEG_HALF = math.exp(-0.5)

CHUNK = 64
GROUP_HEADS = 4
GROUP_LANES = GROUP_HEADS * HEAD_DIM
POOL_HISTORY = 16
CONV_HISTORY = 8
VMEM_LIMIT_BYTES = 56 * 1024 * 1024


def _sigmoid(z):
    return 1.0 / (1.0 + jnp.exp(-z))


def _rms_norm(x, gain):
    ms = jnp.mean(x * x, axis=-1, keepdims=True)
    return x * lax.rsqrt(ms + NORM_EPS) * gain


def _dot(a, b, precision=None):
    return jnp.dot(a, b, preferred_element_type=F32, precision=precision)


def _dot_nt(a, b, precision=None):
    return lax.dot_general(a, b, (((1,), (1,)), ((), ())), preferred_element_type=F32, precision=precision)


def _dot_tn(a, b, precision=None):
    return lax.dot_general(a, b, (((0,), (0,)), ((), ())), preferred_element_type=F32, precision=precision)


def _split_dot(a, ones_bf16):
    hi = a.astype(BF16)
    lo = (a - hi.astype(F32)).astype(BF16)
    return _dot(hi, ones_bf16) + _dot(lo, ones_bf16)


def _mix_in_kernel(x_ref, g1_ref, w_in_ref, mu_ref, w0_ref, wd_ref, a0_ref, wa_ref, wg_ref, kk_ref, ka_ref,
                   wpool_ref, pscale_ref, ones_ref,
                   r_out, k_out, v_out, lw_out, a_out, b_out, g_out, pool_out,
                   ucarry_ref, pcarry_ref, *, tile, rwkv_width, pool_group_dim):
    t = pl.program_id(1)
    rwkv_cols = 3 * rwkv_width + DECAY_LORA + AAA_LORA + GATE_LORA

    @pl.when(t == 0)
    def _():
        ucarry_ref[...] = jnp.zeros_like(ucarry_ref)
        pcarry_ref[...] = jnp.zeros_like(pcarry_ref)

    x = x_ref[0]
    h = _rms_norm(x, g1_ref[...]).astype(BF16)
    proj = _dot(h, w_in_ref[...])

    row = lax.broadcasted_iota(jnp.int32, (tile, 1), 0)

    u = proj[:, :rwkv_cols]
    prev = pltpu.roll(u, 1, 0)
    prev = jnp.where(row == 0, ucarry_ref[...], prev)
    ucarry_ref[...] = u[tile - 1:tile, :]
    u = u + mu_ref[...] * (prev - u)

    w = rwkv_width
    r = u[:, 0:w]
    k = u[:, w:2 * w]
    v = u[:, 2 * w:3 * w]
    o = 3 * w
    wl = u[:, o:o + DECAY_LORA]
    al = u[:, o + DECAY_LORA:o + DECAY_LORA + AAA_LORA]
    gl = u[:, o + DECAY_LORA + AAA_LORA:rwkv_cols]

    z = w0_ref[...] + _dot(jnp.tanh(wl).astype(BF16), wd_ref[...])
    lw = -EXP_NEG_HALF * _sigmoid(z)
    alpha = _sigmoid(a0_ref[...] + _dot(al.astype(BF16), wa_ref[...]))
    g = _dot(_sigmoid(gl).astype(BF16), wg_ref[...])

    kk = k * kk_ref[...]
    kk2 = (kk * kk).astype(BF16)
    gl = GROUP_LANES
    ss = jnp.concatenate([_dot(kk2[:, i * gl:(i + 1) * gl], ones_ref[...]) for i in range(w // gl)], axis=1)
    kk = kk * lax.rsqrt(jnp.maximum(ss, 1e-24))
    k_mod = k * (1.0 + (alpha - 1.0) * ka_ref[...])

    r_out[0] = r
    k_out[0] = k_mod
    v_out[0] = v
    lw_out[0] = lw
    a_out[0] = -kk
    b_out[0] = kk * alpha
    g_out[0] = g

    up = proj[:, rwkv_cols:]
    ext = jnp.concatenate([pcarry_ref[...], up], axis=0)
    pcarry_ref[...] = up[tile - POOL_HISTORY:, :]
    pos = (t * tile + row + 1).astype(F32)
    gd = pool_group_dim
    pooled = []
    for gi, win in enumerate(POOL_WINDOWS):
        s = ext[:, gi * gd:(gi + 1) * gd]
        step = 1
        while step < win:
            s = s + pltpu.roll(s, step, 0)
            step *= 2
        cnt = jnp.minimum(pos, float(win))
        pooled.append((s[POOL_HISTORY:, :] / cnt - up[:, gi * gd:(gi + 1) * gd]).astype(BF16))
    for j in range(len(POOL_WINDOWS) // 2):
        mixed = _dot(jnp.concatenate(pooled[2 * j:2 * j + 2], axis=1), wpool_ref[j])
        pool_out[0, :, 2 * j * gd:2 * (j + 1) * gd] = mixed * pscale_ref[:, 2 * j * gd:2 * (j + 1) * gd]


def _mix_in(x, g1, w_in, mu, w0, wd, a0, wa, wg, k_k, k_a, w_pool, pool_scale, ones, *, tile):
    bsz, t_len, d_model = x.shape
    rwkv_width = w0.shape[-1]
    pool_width = pool_scale.shape[-1]
    pool_group_dim = w_pool.shape[-1] // 2
    rwkv_cols = mu.shape[-1]
    grid = (bsz, t_len // tile)

    def full(a):
        nd = a.ndim
        return pl.BlockSpec(a.shape, lambda b, t, _nd=nd: (0,) * _nd)

    seq_spec = lambda width: pl.BlockSpec((1, tile, width), lambda b, t: (b, t, 0))
    out_sds = lambda width: jax.ShapeDtypeStruct((bsz, t_len, width), F32)
    params = (g1, w_in, mu, w0, wd, a0, wa, wg, k_k, k_a, w_pool, pool_scale, ones)
    kern = functools.partial(_mix_in_kernel, tile=tile, rwkv_width=rwkv_width, pool_group_dim=pool_group_dim)
    return pl.pallas_call(
        kern,
        grid=grid,
        in_specs=[seq_spec(d_model)] + [full(p) for p in params],
        out_specs=[seq_spec(rwkv_width)] * 7 + [seq_spec(pool_width)],
        out_shape=[out_sds(rwkv_width)] * 7 + [out_sds(pool_width)],
        scratch_shapes=[pltpu.VMEM((1, rwkv_cols), F32), pltpu.VMEM((POOL_HISTORY, pool_width), F32)],
        compiler_params=pltpu.CompilerParams(dimension_semantics=("arbitrary", "arbitrary"),
                                             vmem_limit_bytes=VMEM_LIMIT_BYTES),
        name="mix_in",
    )(x, *params)


def _block_diag(x, block_mask):
    return jnp.where(block_mask, jnp.concatenate([x] * GROUP_HEADS, axis=0), jnp.zeros((), x.dtype))


def _split3_dot(m3_bf16, a):
    hi = a.astype(BF16)
    r1 = a - hi.astype(F32)
    mid = r1.astype(BF16)
    lo = (r1 - mid.astype(F32)).astype(BF16)
    return _dot(m3_bf16, jnp.concatenate([hi, mid, lo], axis=0))


def _rwkv_kernel(r_ref, k_ref, v_ref, lw_ref, a_ref, b_ref, g_ref, rk_ref, lnw_ref, lnb_ref, y_ref, s_ref, tr_ref,
                 *, batch_block, groups, chunks, chunk_skew):
    t = pl.program_id(1)

    @pl.when(t == 0)
    def _():
        s_ref[...] = jnp.zeros_like(s_ref)

    c_len, gl = CHUNK, GROUP_LANES
    row = lax.broadcasted_iota(jnp.int32, (c_len, gl), 0)
    lane_pos = lax.broadcasted_iota(jnp.int32, (c_len, gl), 1) % c_len
    strict = lane_pos < row
    incl = lane_pos <= row
    eye = (lane_pos == row).astype(F32)
    bm_r = lax.broadcasted_iota(jnp.int32, (gl, gl), 0) // HEAD_DIM
    bm_c = lax.broadcasted_iota(jnp.int32, (gl, gl), 1) // HEAD_DIM
    block_mask = bm_r == bm_c
    head_ones = block_mask.astype(BF16)
    tri = (lax.broadcasted_iota(jnp.int32, (c_len, c_len), 1)
           <= lax.broadcasted_iota(jnp.int32, (c_len, c_len), 0)).astype(BF16)
    tri3 = jnp.concatenate([tri, tri, tri], axis=1)
    bd = lambda m: _block_diag(m.astype(BF16), block_mask)
    bf = lambda m: m.astype(BF16)

    def chunk_stages(bi, gi, c):
        rows = slice(c * c_len, (c + 1) * c_len)
        lanes = slice(gi * gl, (gi + 1) * gl)
        si = bi * groups + gi
        lw = lw_ref[bi, rows, lanes]
        cl = _split3_dot(tri3, lw)
        yield

        cl_last = cl[c_len - 1:c_len, :]
        a = a_ref[bi, rows, lanes]
        b = b_ref[bi, rows, lanes]
        r = r_ref[bi, rows, lanes]
        k = k_ref[bi, rows, lanes]
        e_neg = jnp.exp(-cl)
        e_end = jnp.exp(cl_last - cl)
        bh = b * e_end
        kh = k * e_end
        p_end = jnp.exp(cl_last)
        ar = bf(jnp.concatenate([a * jnp.exp(cl - lw), r * jnp.exp(cl)], axis=0))
        tr_ref[2 * si] = bd(b * e_neg).T
        tr_ref[2 * si + 1] = bd(k * e_neg).T
        ab = _dot(ar, tr_ref[2 * si])
        ak = _dot(ar, tr_ref[2 * si + 1])
        yield

        a_ab = jnp.where(strict, ab[:c_len], 0.0)
        a_rb = jnp.where(incl, ab[c_len:], 0.0)
        a_ak = jnp.where(strict, ak[:c_len], 0.0)
        a_rk = jnp.where(incl, ak[c_len:], 0.0)
        lp = _dot(bf(a_ab), bd(a_ab))
        tm = eye + a_ab
        yield
        n_sq = int(math.log2(c_len)) - 1
        for j in range(1, n_sq + 1):
            wmat = bd(lp)
            if j < n_sq:
                res = _dot(bf(jnp.concatenate([lp, tm], axis=0)), wmat)
                lp = res[:c_len]
                tm = tm + res[c_len:]
            else:
                tm = tm + _dot(bf(tm), wmat)
            yield

        v = v_ref[bi, rows, lanes]
        s_bd = s_ref[si]
        sar = _dot_nt(ar, bf(s_bd))
        av = _dot(bf(jnp.concatenate([a_ak, a_rk], axis=0)), bd(v))
        yield

        u = _dot(bf(tm), bd(sar[:c_len] + av[:c_len]))
        yield

        y = sar[c_len:] + av[c_len:] + _dot(bf(a_rb), bd(u))
        upd = _dot_tn(bf(jnp.concatenate([u, v], axis=0)), bf(jnp.concatenate([bh, kh], axis=0)))
        yield

        s_ref[si] = s_bd * p_end + jnp.where(block_mask, upd, 0.0)
        sums = yield jnp.concatenate([y, r * k * rk_ref[:, lanes]], axis=0)
        mean = sums[:c_len] * (1.0 / HEAD_DIM)
        bonus = sums[c_len:] * v
        d = y - mean
        var = (yield d * d) * (1.0 / HEAD_DIM)
        yn = d * lax.rsqrt(var + GN_EPS) * lnw_ref[:, lanes] + lnb_ref[:, lanes]
        y_ref[bi, rows, lanes] = (yn + bonus) * g_ref[bi, rows, lanes]

    def advance(streams, asked):
        if asked[0] is None:
            replies = [None] * len(streams)
        else:
            n_rows = asked[0].shape[0]
            sums = _dot(bf(jnp.concatenate(asked, axis=0)), head_ones)
            replies = [sums[i * n_rows:(i + 1) * n_rows] for i in range(len(streams))]
        alive, new_asked = [], []
        for s, reply in zip(streams, replies):
            try:
                new_asked.append(s.send(reply))
                alive.append(s)
            except StopIteration:
                pass
        return alive, new_asked

    in_flight = {}
    started, tick = 0, 0
    while started < chunks or in_flight:
        for c in sorted(in_flight):
            streams, asked = advance(*in_flight[c])
            if streams:
                in_flight[c] = (streams, asked)
            else:
                del in_flight[c]
        if started < chunks and tick >= started * chunk_skew:
            streams = [chunk_stages(bi, gi, started) for bi in range(batch_block) for gi in range(groups)]
            in_flight[started] = (streams, [next(s) for s in streams])
            started += 1
        tick += 1


def _rwkv(r, k, v, lw, a, b, g, r_k, ln_w, ln_b, *, tile, batch_block):
    bsz, t_len, width = r.shape
    groups = width // GROUP_LANES
    chunks = tile // CHUNK
    grid = (bsz // batch_block, t_len // tile)
    seq_spec = pl.BlockSpec((batch_block, tile, width), lambda bi, ti: (bi, ti, 0))
    par_spec = pl.BlockSpec((1, width), lambda bi, ti: (0, 0))
    return pl.pallas_call(
        functools.partial(_rwkv_kernel, batch_block=batch_block, groups=groups, chunks=chunks, chunk_skew=6),
        grid=grid,
        in_specs=[seq_spec] * 7 + [par_spec] * 3,
        out_specs=seq_spec,
        out_shape=jax.ShapeDtypeStruct((bsz, t_len, width), F32),
        scratch_shapes=[pltpu.VMEM((batch_block * groups, GROUP_LANES, GROUP_LANES), F32),
                        pltpu.VMEM((2 * batch_block * groups, GROUP_LANES, GROUP_LANES), BF16)],
        compiler_params=pltpu.CompilerParams(dimension_semantics=("arbitrary", "arbitrary"),
                                             vmem_limit_bytes=VMEM_LIMIT_BYTES),
        name="rwkv",
    )(r, k, v, lw, a, b, g, r_k, ln_w, ln_b)


def _gelu_tanh(x):
    c1 = -2.0 * math.sqrt(2.0 / math.pi) * math.log2(math.e)
    c2 = c1 * 0.044715
    return x / (1.0 + jnp.exp2(x * (c1 + c2 * (x * x))))


def _mix_out_kernel(x_ref, yr_ref, yp_ref, w_out_ref, g2_ref, g3_ref, w_up_ref, cw_ref, cb_ref, w_down_ref, g4_ref,
                    out_ref, ucarry_ref, act_ref, *, tile, d_ff, col_chunk):
    t = pl.program_id(1)

    @pl.when(t == 0)
    def _():
        ucarry_ref[...] = jnp.zeros_like(ucarry_ref)

    ycat = jnp.concatenate([yr_ref[0], yp_ref[0]], axis=-1).astype(BF16)
    mixed = _dot(ycat, w_out_ref[...])
    x1 = x_ref[0] + _rms_norm(mixed, g2_ref[...])
    h2 = _rms_norm(x1, g3_ref[...]).astype(BF16)

    def conv_cols(c0):
        cols = slice(c0, c0 + col_chunk)
        up = _dot(h2, w_up_ref[:, cols])
        ext = jnp.concatenate([ucarry_ref[:, cols], up], axis=0)
        ucarry_ref[:, cols] = up[tile - CONV_HISTORY:, :]
        s1 = pltpu.roll(ext, 1, 0)[CONV_HISTORY:, :]
        s2 = pltpu.roll(ext, 2, 0)[CONV_HISTORY:, :]
        return (up * cw_ref[2:3, cols] + s1 * cw_ref[1:2, cols] + s2 * cw_ref[0:1, cols]) + cb_ref[:, cols]

    for c in range(d_ff // col_chunk):
        c0 = c * col_chunk
        act = _gelu_tanh(conv_cols(c0)) * conv_cols(d_ff + c0)
        act_ref[:, c0:c0 + col_chunk] = act.astype(BF16)

    f = _dot(act_ref[...], w_down_ref[...])
    out_ref[0] = x1 + _rms_norm(f, g4_ref[...])


def _mix_out(x, y_rwkv, y_pool, w_out, g2, g3, w_up, conv_w, conv_b, w_down, g4, *, tile, col_chunk):
    bsz, t_len, d_model = x.shape
    d_ff = w_down.shape[0]
    grid = (bsz, t_len // tile)

    def full(a):
        nd = a.ndim
        return pl.BlockSpec(a.shape, lambda b, t, _nd=nd: (0,) * _nd, pipeline_mode=pl.Buffered(1))

    seq_spec = lambda width: pl.BlockSpec((1, tile, width), lambda b, t: (b, t, 0))
    params = (w_out, g2, g3, w_up, conv_w, conv_b, w_down, g4)
    kern = functools.partial(_mix_out_kernel, tile=tile, d_ff=d_ff, col_chunk=col_chunk)
    return pl.pallas_call(
        kern,
        grid=grid,
        in_specs=[seq_spec(d_model), seq_spec(y_rwkv.shape[-1]), seq_spec(y_pool.shape[-1])]
                 + [full(p) for p in params],
        out_specs=seq_spec(d_model),
        out_shape=jax.ShapeDtypeStruct((bsz, t_len, d_model), F32),
        scratch_shapes=[pltpu.VMEM((CONV_HISTORY, 2 * d_ff), F32), pltpu.VMEM((tile, d_ff), BF16)],
        compiler_params=pltpu.CompilerParams(dimension_semantics=("arbitrary", "arbitrary"),
                                             vmem_limit_bytes=VMEM_LIMIT_BYTES),
        name="mix_out",
    )(x, y_rwkv, y_pool, *params)


def _layer(x, pre_mix_norm, w_in, mu_shift, w0, w_decay_up, a0, w_aaa_up, w_gate_up, k_k, k_a, r_k, ln_x_w, ln_x_b,
           w_pool, pool_scale, w_out, post_mix_norm, pre_ffn_norm, w_up, conv_w, conv_b, w_down, post_ffn_norm,
           *, in_tile, rwkv_tile, rwkv_batch_block, out_tile, col_chunk):
    row = lambda p: p.reshape(1, -1).astype(F32)
    head_id = jnp.arange(GROUP_LANES) // HEAD_DIM
    ones = (head_id[:, None] == head_id[None, :]).astype(BF16)
    n_groups, gd, _ = w_pool.shape
    wp = w_pool.astype(BF16).reshape(n_groups // 2, 2, gd, gd)
    zero = jnp.zeros_like(wp[:, 0])
    w_pool_pairs = jnp.concatenate([jnp.concatenate([wp[:, 0], zero], axis=2),
                                    jnp.concatenate([zero, wp[:, 1]], axis=2)], axis=1)

    r, k, v, lw, a, b, g, y_pool = _mix_in(
        x, row(pre_mix_norm), w_in.astype(BF16), row(mu_shift), row(w0), w_decay_up.astype(BF16), row(a0),
        w_aaa_up.astype(BF16), w_gate_up.astype(BF16), row(k_k), row(k_a), w_pool_pairs, row(pool_scale),
        ones, tile=in_tile)
    y_rwkv = _rwkv(r, k, v, lw, a, b, g, row(r_k), row(ln_x_w), row(ln_x_b), tile=rwkv_tile,
                   batch_block=rwkv_batch_block)
    return _mix_out(x, y_rwkv, y_pool, w_out.astype(BF16), row(post_mix_norm), row(pre_ffn_norm),
                    w_up.astype(BF16), conv_w.astype(F32), row(conv_b), w_down.astype(BF16), row(post_ffn_norm),
                    tile=out_tile, col_chunk=col_chunk)


def kernel(x, pre_mix_norm, w_in, mu_shift, w0, w_decay_up, a0, w_aaa_up, w_gate_up, k_k, k_a, r_k, ln_x_w, ln_x_b,
           w_pool, pool_scale, w_out, post_mix_norm, pre_ffn_norm, w_up, conv_w, conv_b, w_down, post_ffn_norm):
    depth = w_in.shape[0]
    for layer in range(depth):
        x = _layer(x, pre_mix_norm[layer], w_in[layer], mu_shift[layer], w0[layer], w_decay_up[layer], a0[layer],
                   w_aaa_up[layer], w_gate_up[layer], k_k[layer], k_a[layer], r_k[layer], ln_x_w[layer],
                   ln_x_b[layer], w_pool[layer], pool_scale[layer], w_out[layer], post_mix_norm[layer],
                   pre_ffn_norm[layer], w_up[layer], conv_w[layer], conv_b[layer], w_down[layer],
                   post_ffn_norm[layer], in_tile=512, rwkv_tile=256, rwkv_batch_block=4,
                   out_tile=512, col_chunk=256)
    return x
```

```python
import functools
import math

import jax
import jax.numpy as jnp
from jax import lax
from jax.experimental import pallas as pl
from jax.experimental.pallas import tpu as pltpu

F32 = jnp.float32
BF16 = jnp.bfloat16

HEAD_DIM = 64
POOL_WINDOWS = (2, 4, 8, 16)
DECAY_LORA = 64
AAA_LORA = 64
GATE_LORA = 128
CONV_WIDTH = 3
NORM_EPS = 1e-6
GN_EPS = 64e-5
EXP_NEG_HALF = math.exp(-0.5)

CHUNK = 64
GROUP_HEADS = 4
GROUP_LANES = GROUP_HEADS * HEAD_DIM
POOL_HISTORY = 16
CONV_HISTORY = 8
VMEM_LIMIT_BYTES = 56 * 1024 * 1024


def _sigmoid(z):
    return 1.0 / (1.0 + jnp.exp(-z))


def _rms_norm(x, gain):
    ms = jnp.mean(x * x, axis=-1, keepdims=True)
    return x * lax.rsqrt(ms + NORM_EPS) * gain


def _dot(a, b, precision=None):
    return jnp.dot(a, b, preferred_element_type=F32, precision=precision)


def _dot_nt(a, b, precision=None):
    return lax.dot_general(a, b, (((1,), (1,)), ((), ())), preferred_element_type=F32, precision=precision)


def _dot_tn(a, b, precision=None):
    return lax.dot_general(a, b, (((0,), (0,)), ((), ())), preferred_element_type=F32, precision=precision)


def _split_dot(a, ones_bf16):
    hi = a.astype(BF16)
    lo = (a - hi.astype(F32)).astype(BF16)
    return _dot(hi, ones_bf16) + _dot(lo, ones_bf16)


def _mix_in_kernel(x_ref, g1_ref, w_in_ref, mu_ref, w0_ref, wd_ref, a0_ref, wa_ref, wg_ref, kk_ref, ka_ref,
                   wpool_ref, pscale_ref, ones_ref,
                   r_out, k_out, v_out, lw_out, a_out, b_out, g_out, pool_out,
                   ucarry_ref, pcarry_ref, proj_ref, *, tile, sub_tiles, rwkv_width, pool_group_dim):
    t = pl.program_id(1)

    @pl.when(t == 0)
    def _():
        ucarry_ref[...] = jnp.zeros_like(ucarry_ref)
        pcarry_ref[...] = jnp.zeros_like(pcarry_ref)

    starts = [sum(sub_tiles[:i]) for i in range(len(sub_tiles))]
    assert sum(sub_tiles) == tile
    for r0, n in zip(starts, sub_tiles):
        rows = slice(r0, r0 + n)
        h = _rms_norm(x_ref[0, rows, :], g1_ref[...]).astype(BF16)
        proj_ref[rows, :] = _dot(h, w_in_ref[...])

    outs = (r_out, k_out, v_out, lw_out, a_out, b_out, g_out, pool_out)
    params = (mu_ref, w0_ref, wd_ref, a0_ref, wa_ref, wg_ref, kk_ref, ka_ref, wpool_ref, pscale_ref, ones_ref)
    for r0, n in zip(starts, sub_tiles):
        rows = slice(r0, r0 + n)
        _mix_in_rows(proj_ref[rows, :], t * tile + r0, rows, params, outs, ucarry_ref, pcarry_ref,
                     rwkv_width=rwkv_width, pool_group_dim=pool_group_dim)


def _mix_in_rows(proj, pos0, rows, params, outs, ucarry_ref, pcarry_ref, *, rwkv_width, pool_group_dim):
    mu_ref, w0_ref, wd_ref, a0_ref, wa_ref, wg_ref, kk_ref, ka_ref, wpool_ref, pscale_ref, ones_ref = params
    r_out, k_out, v_out, lw_out, a_out, b_out, g_out, pool_out = outs
    n = proj.shape[0]
    rwkv_cols = 3 * rwkv_width + DECAY_LORA + AAA_LORA + GATE_LORA
    row = lax.broadcasted_iota(jnp.int32, (n, 1), 0)

    u = proj[:, :rwkv_cols]
    prev = pltpu.roll(u, 1, 0)
    prev = jnp.where(row == 0, ucarry_ref[...], prev)
    ucarry_ref[...] = u[n - 1:n, :]
    u = u + mu_ref[...] * (prev - u)

    w = rwkv_width
    r = u[:, 0:w]
    k = u[:, w:2 * w]
    v = u[:, 2 * w:3 * w]
    o = 3 * w
    wl = u[:, o:o + DECAY_LORA]
    al = u[:, o + DECAY_LORA:o + DECAY_LORA + AAA_LORA]
    gl = u[:, o + DECAY_LORA + AAA_LORA:rwkv_cols]

    z = w0_ref[...] + _dot(jnp.tanh(wl).astype(BF16), wd_ref[...])
    lw = -EXP_NEG_HALF * _sigmoid(z)
    alpha = _sigmoid(a0_ref[...] + _dot(al.astype(BF16), wa_ref[...]))
    g = _dot(_sigmoid(gl).astype(BF16), wg_ref[...])

    kk = k * kk_ref[...]
    kk2 = (kk * kk).astype(BF16)
    gl = GROUP_LANES
    ss = jnp.concatenate([_dot(kk2[:, i * gl:(i + 1) * gl], ones_ref[...]) for i in range(w // gl)], axis=1)
    kk = kk * lax.rsqrt(jnp.maximum(ss, 1e-24))
    k_mod = k * (1.0 + (alpha - 1.0) * ka_ref[...])

    r_out[0, rows, :] = r
    k_out[0, rows, :] = k_mod
    v_out[0, rows, :] = v
    lw_out[0, rows, :] = lw
    a_out[0, rows, :] = -kk
    b_out[0, rows, :] = kk * alpha
    g_out[0, rows, :] = g

    up = proj[:, rwkv_cols:]
    ext = jnp.concatenate([pcarry_ref[...], up], axis=0)
    pcarry_ref[...] = up[n - POOL_HISTORY:, :]
    pos = (pos0 + row + 1).astype(F32)
    gd = pool_group_dim
    pooled = []
    for gi, win in enumerate(POOL_WINDOWS):
        s = ext[:, gi * gd:(gi + 1) * gd]
        step = 1
        while step < win:
            s = s + pltpu.roll(s, step, 0)
            step *= 2
        cnt = jnp.minimum(pos, float(win))
        pooled.append((s[POOL_HISTORY:, :] / cnt - up[:, gi * gd:(gi + 1) * gd]).astype(BF16))
    for j in range(len(POOL_WINDOWS) // 2):
        mixed = _dot(jnp.concatenate(pooled[2 * j:2 * j + 2], axis=1), wpool_ref[j])
        pool_out[0, rows, 2 * j * gd:2 * (j + 1) * gd] = (
            mixed * pscale_ref[:, 2 * j * gd:2 * (j + 1) * gd]).astype(BF16)


def _mix_in(x, g1, w_in, mu, w0, wd, a0, wa, wg, k_k, k_a, w_pool, pool_scale, ones, *, tile, sub_tiles):
    bsz, t_len, d_model = x.shape
    rwkv_width = w0.shape[-1]
    pool_width = pool_scale.shape[-1]
    pool_group_dim = w_pool.shape[-1] // 2
    rwkv_cols = mu.shape[-1]
    grid = (bsz, t_len // tile)

    def full(a):
        nd = a.ndim
        return pl.BlockSpec(a.shape, lambda b, t, _nd=nd: (0,) * _nd)

    seq_spec = lambda width: pl.BlockSpec((1, tile, width), lambda b, t: (b, t, 0))
    out_sds = lambda width, dtype=F32: jax.ShapeDtypeStruct((bsz, t_len, width), dtype)
    params = (g1, w_in, mu, w0, wd, a0, wa, wg, k_k, k_a, w_pool, pool_scale, ones)
    kern = functools.partial(_mix_in_kernel, tile=tile, sub_tiles=sub_tiles, rwkv_width=rwkv_width,
                             pool_group_dim=pool_group_dim)
    return pl.pallas_call(
        kern,
        grid=grid,
        in_specs=[seq_spec(d_model)] + [full(p) for p in params],
        out_specs=[seq_spec(rwkv_width)] * 7 + [seq_spec(pool_width)],
        out_shape=[out_sds(rwkv_width)] * 7 + [out_sds(pool_width, BF16)],
        scratch_shapes=[pltpu.VMEM((1, rwkv_cols), F32), pltpu.VMEM((POOL_HISTORY, pool_width), F32),
                        pltpu.VMEM((tile, w_in.shape[-1]), F32)],
        compiler_params=pltpu.CompilerParams(dimension_semantics=("arbitrary", "arbitrary"),
                                             vmem_limit_bytes=VMEM_LIMIT_BYTES),
        name="mix_in",
    )(x, *params)


def _block_diag(x, block_mask):
    return jnp.where(block_mask, jnp.concatenate([x] * GROUP_HEADS, axis=0), jnp.zeros((), x.dtype))


def _split3_dot(m3_bf16, a):
    hi = a.astype(BF16)
    r1 = a - hi.astype(F32)
    mid = r1.astype(BF16)
    lo = (r1 - mid.astype(F32)).astype(BF16)
    return _dot(m3_bf16, jnp.concatenate([hi, mid, lo], axis=0))


def _rwkv_kernel(r_ref, k_ref, v_ref, lw_ref, a_ref, b_ref, g_ref, rk_ref, lnw_ref, lnb_ref, y_ref, s_ref, tr_ref,
                 *, batch_block, groups, chunks, chunk_skew):
    t = pl.program_id(1)

    @pl.when(t == 0)
    def _():
        s_ref[...] = jnp.zeros_like(s_ref)

    c_len, gl = CHUNK, GROUP_LANES
    row = lax.broadcasted_iota(jnp.int32, (c_len, gl), 0)
    lane_pos = lax.broadcasted_iota(jnp.int32, (c_len, gl), 1) % c_len
    strict = lane_pos < row
    incl = lane_pos <= row
    eye = (lane_pos == row).astype(F32)
    bm_r = lax.broadcasted_iota(jnp.int32, (gl, gl), 0) // HEAD_DIM
    bm_c = lax.broadcasted_iota(jnp.int32, (gl, gl), 1) // HEAD_DIM
    block_mask = bm_r == bm_c
    head_ones = block_mask.astype(BF16)
    tri = (lax.broadcasted_iota(jnp.int32, (c_len, c_len), 1)
           <= lax.broadcasted_iota(jnp.int32, (c_len, c_len), 0)).astype(BF16)
    tri3 = jnp.concatenate([tri, tri, tri], axis=1)
    bd = lambda m: _block_diag(m.astype(BF16), block_mask)
    bf = lambda m: m.astype(BF16)

    def chunk_stages(bi, gi, c):
        rows = slice(c * c_len, (c + 1) * c_len)
        lanes = slice(gi * gl, (gi + 1) * gl)
        si = bi * groups + gi
        lw = lw_ref[bi, rows, lanes]
        cl = _split3_dot(tri3, lw)
        yield

        cl_last = cl[c_len - 1:c_len, :]
        a = a_ref[bi, rows, lanes]
        b = b_ref[bi, rows, lanes]
        r = r_ref[bi, rows, lanes]
        k = k_ref[bi, rows, lanes]
        e_neg = jnp.exp(-cl)
        e_end = jnp.exp(cl_last - cl)
        bh = b * e_end
        kh = k * e_end
        p_end = jnp.exp(cl_last)
        ar = bf(jnp.concatenate([a * jnp.exp(cl - lw), r * jnp.exp(cl)], axis=0))
        tr_ref[2 * si] = bd(b * e_neg).T
        tr_ref[2 * si + 1] = bd(k * e_neg).T
        ab = _dot(ar, tr_ref[2 * si])
        ak = _dot(ar, tr_ref[2 * si + 1])
        yield

        a_ab = jnp.where(strict, ab[:c_len], 0.0)
        a_rb = jnp.where(incl, ab[c_len:], 0.0)
        a_ak = jnp.where(strict, ak[:c_len], 0.0)
        a_rk = jnp.where(incl, ak[c_len:], 0.0)
        lp = _dot(bf(a_ab), bd(a_ab))
        tm = eye + a_ab
        yield
        n_sq = int(math.log2(c_len)) - 1
        for j in range(1, n_sq + 1):
            wmat = bd(lp)
            if j < n_sq:
                res = _dot(bf(jnp.concatenate([lp, tm], axis=0)), wmat)
                lp = res[:c_len]
                tm = tm + res[c_len:]
            else:
                tm = tm + _dot(bf(tm), wmat)
            yield

        v = v_ref[bi, rows, lanes]
        s_bd = s_ref[si]
        sar = _dot_nt(ar, bf(s_bd))
        av = _dot(bf(jnp.concatenate([a_ak, a_rk], axis=0)), bd(v))
        yield

        u = _dot(bf(tm), bd(sar[:c_len] + av[:c_len]))
        yield

        y = sar[c_len:] + av[c_len:] + _dot(bf(a_rb), bd(u))
        upd = _dot_tn(bf(jnp.concatenate([u, v], axis=0)), bf(jnp.concatenate([bh, kh], axis=0)))
        yield

        s_ref[si] = s_bd * p_end + jnp.where(block_mask, upd, 0.0)
        sums = yield jnp.concatenate([y, r * k * rk_ref[:, lanes]], axis=0)
        mean = sums[:c_len] * (1.0 / HEAD_DIM)
        bonus = sums[c_len:] * v
        d = y - mean
        var = (yield d * d) * (1.0 / HEAD_DIM)
        yn = d * lax.rsqrt(var + GN_EPS) * lnw_ref[:, lanes] + lnb_ref[:, lanes]
        y_ref[bi, rows, lanes] = ((yn + bonus) * g_ref[bi, rows, lanes]).astype(BF16)

    def advance(streams, asked):
        if asked[0] is None:
            replies = [None] * len(streams)
        else:
            n_rows = asked[0].shape[0]
            sums = _dot(bf(jnp.concatenate(asked, axis=0)), head_ones)
            replies = [sums[i * n_rows:(i + 1) * n_rows] for i in range(len(streams))]
        alive, new_asked = [], []
        for s, reply in zip(streams, replies):
            try:
                new_asked.append(s.send(reply))
                alive.append(s)
            except StopIteration:
                pass
        return alive, new_asked

    in_flight = {}
    started, tick = 0, 0
    while started < chunks or in_flight:
        for c in sorted(in_flight):
            streams, asked = advance(*in_flight[c])
            if streams:
                in_flight[c] = (streams, asked)
            else:
                del in_flight[c]
        if started < chunks and tick >= started * chunk_skew:
            streams = [chunk_stages(bi, gi, started) for bi in range(batch_block) for gi in range(groups)]
            in_flight[started] = (streams, [next(s) for s in streams])
            started += 1
        tick += 1


def _rwkv(r, k, v, lw, a, b, g, r_k, ln_w, ln_b, *, tile, batch_block):
    bsz, t_len, width = r.shape
    groups = width // GROUP_LANES
    chunks = tile // CHUNK
    grid = (bsz // batch_block, t_len // tile)
    seq_spec = pl.BlockSpec((batch_block, tile, width), lambda bi, ti: (bi, ti, 0))
    par_spec = pl.BlockSpec((1, width), lambda bi, ti: (0, 0))
    return pl.pallas_call(
        functools.partial(_rwkv_kernel, batch_block=batch_block, groups=groups, chunks=chunks, chunk_skew=6),
        grid=grid,
        in_specs=[seq_spec] * 7 + [par_spec] * 3,
        out_specs=seq_spec,
        out_shape=jax.ShapeDtypeStruct((bsz, t_len, width), BF16),
        scratch_shapes=[pltpu.VMEM((batch_block * groups, GROUP_LANES, GROUP_LANES), F32),
                        pltpu.VMEM((2 * batch_block * groups, GROUP_LANES, GROUP_LANES), BF16)],
        compiler_params=pltpu.CompilerParams(dimension_semantics=("arbitrary", "arbitrary"),
                                             vmem_limit_bytes=VMEM_LIMIT_BYTES),
        name="rwkv",
    )(r, k, v, lw, a, b, g, r_k, ln_w, ln_b)


def _gelu_tanh(x):
    c1 = -2.0 * math.sqrt(2.0 / math.pi) * math.log2(math.e)
    c2 = c1 * 0.044715
    return x / (1.0 + jnp.exp2(x * (c1 + c2 * (x * x))))


def _mix_out_kernel(x_ref, yr_ref, yp_ref, w_out_ref, g2_ref, g3_ref, w_up_ref, cw_ref, cb_ref, w_down_ref, g4_ref,
                    out_ref, ucarry_ref, h2_ref, act_ref, *, tile, sub_tile, d_ff, col_chunk, down_at):
    t = pl.program_id(1)
    n_sub = tile // sub_tile
    n_chunks = d_ff // col_chunk

    @pl.when(t == 0)
    def _():
        ucarry_ref[...] = jnp.zeros_like(ucarry_ref)

    for i in range(n_sub):
        rows = slice(i * sub_tile, (i + 1) * sub_tile)
        ycat = jnp.concatenate([yr_ref[0, rows, :], yp_ref[0, rows, :]], axis=-1)
        x1 = x_ref[0, rows, :] + _rms_norm(_dot(ycat, w_out_ref[...]), g2_ref[...])
        out_ref[0, rows, :] = x1
        h2_ref[rows, :] = _rms_norm(x1, g3_ref[...]).astype(BF16)

    def up_proj(i, c):
        h2 = h2_ref[i * sub_tile:(i + 1) * sub_tile, :]
        return [_dot(h2, w_up_ref[:, c0:c0 + col_chunk]) for c0 in (c * col_chunk, d_ff + c * col_chunk)]

    def conv(up, c0):
        cols = slice(c0, c0 + col_chunk)
        ext = jnp.concatenate([ucarry_ref[:, cols], up], axis=0)
        ucarry_ref[:, cols] = up[sub_tile - CONV_HISTORY:, :]
        s1 = pltpu.roll(ext, 1, 0)[CONV_HISTORY:, :]
        s2 = pltpu.roll(ext, 2, 0)[CONV_HISTORY:, :]
        return (up * cw_ref[2:3, cols] + s1 * cw_ref[1:2, cols] + s2 * cw_ref[0:1, cols]) + cb_ref[:, cols]

    d_model = w_down_ref.shape[-1]
    n_out_blocks = len(down_at)
    out_block = d_model // n_out_blocks

    def act_rows(i):
        slot = i % 2
        return slice(slot * sub_tile, (slot + 1) * sub_tile)

    def down_block(i, j):
        return _dot(act_ref[act_rows(i), :], w_down_ref[:, j * out_block:(j + 1) * out_block])

    def finish(i, f_blocks):
        rows = slice(i * sub_tile, (i + 1) * sub_tile)
        f = jnp.concatenate(f_blocks, axis=1)
        out_ref[0, rows, :] = out_ref[0, rows, :] + _rms_norm(f, g4_ref[...])

    for i in range(n_sub):
        rows = act_rows(i)
        f_blocks = []
        for c in range(n_chunks):
            gate_up, val_up = up_proj(i, c)
            if i > 0 and c in down_at:
                f_blocks.append(down_block(i - 1, down_at.index(c)))
                if len(f_blocks) == n_out_blocks:
                    finish(i - 1, f_blocks)
            c0 = c * col_chunk
            act = _gelu_tanh(conv(gate_up, c0)) * conv(val_up, d_ff + c0)
            act_ref[rows, c0:c0 + col_chunk] = act.astype(BF16)
    finish(n_sub - 1, [down_block(n_sub - 1, j) for j in range(n_out_blocks)])


def _mix_out(x, y_rwkv, y_pool, w_out, g2, g3, w_up, conv_w, conv_b, w_down, g4, *, tile, sub_tile, col_chunk):
    bsz, t_len, d_model = x.shape
    d_ff = w_down.shape[0]
    grid = (bsz, t_len // tile)

    def full(a):
        nd = a.ndim
        return pl.BlockSpec(a.shape, lambda b, t, _nd=nd: (0,) * _nd, pipeline_mode=pl.Buffered(1))

    seq_spec = lambda width: pl.BlockSpec((1, tile, width), lambda b, t: (b, t, 0))
    params = (w_out, g2, g3, w_up, conv_w, conv_b, w_down, g4)
    kern = functools.partial(_mix_out_kernel, tile=tile, sub_tile=sub_tile, d_ff=d_ff, col_chunk=col_chunk,
                             down_at=(4, 6, 8, 10))
    return pl.pallas_call(
        kern,
        grid=grid,
        in_specs=[seq_spec(d_model), seq_spec(y_rwkv.shape[-1]), seq_spec(y_pool.shape[-1])]
                 + [full(p) for p in params],
        out_specs=seq_spec(d_model),
        out_shape=jax.ShapeDtypeStruct((bsz, t_len, d_model), F32),
        scratch_shapes=[pltpu.VMEM((CONV_HISTORY, 2 * d_ff), F32), pltpu.VMEM((tile, d_model), BF16),
                        pltpu.VMEM((2 * sub_tile, d_ff), BF16)],
        compiler_params=pltpu.CompilerParams(dimension_semantics=("arbitrary", "arbitrary"),
                                             vmem_limit_bytes=VMEM_LIMIT_BYTES),
        name="mix_out",
    )(x, y_rwkv, y_pool, *params)


def _layer(x, pre_mix_norm, w_in, mu_shift, w0, w_decay_up, a0, w_aaa_up, w_gate_up, k_k, k_a, r_k, ln_x_w, ln_x_b,
           w_pool, pool_scale, w_out, post_mix_norm, pre_ffn_norm, w_up, conv_w, conv_b, w_down, post_ffn_norm,
           *, in_tile, in_sub_tiles, rwkv_tile, rwkv_batch_block, out_tile, out_sub_tile, col_chunk):
    row = lambda p: p.reshape(1, -1).astype(F32)
    head_id = jnp.arange(GROUP_LANES) // HEAD_DIM
    ones = (head_id[:, None] == head_id[None, :]).astype(BF16)
    n_groups, gd, _ = w_pool.shape
    wp = w_pool.astype(BF16).reshape(n_groups // 2, 2, gd, gd)
    zero = jnp.zeros_like(wp[:, 0])
    w_pool_pairs = jnp.concatenate([jnp.concatenate([wp[:, 0], zero], axis=2),
                                    jnp.concatenate([zero, wp[:, 1]], axis=2)], axis=1)

    r, k, v, lw, a, b, g, y_pool = _mix_in(
        x, row(pre_mix_norm), w_in.astype(BF16), row(mu_shift), row(w0), w_decay_up.astype(BF16), row(a0),
        w_aaa_up.astype(BF16), w_gate_up.astype(BF16), row(k_k), row(k_a), w_pool_pairs, row(pool_scale),
        ones, tile=in_tile, sub_tiles=in_sub_tiles)
    y_rwkv = _rwkv(r, k, v, lw, a, b, g, row(r_k), row(ln_x_w), row(ln_x_b), tile=rwkv_tile,
                   batch_block=rwkv_batch_block)
    return _mix_out(x, y_rwkv, y_pool, w_out.astype(BF16), row(post_mix_norm), row(pre_ffn_norm),
                    w_up.astype(BF16), conv_w.astype(F32), row(conv_b), w_down.astype(BF16), row(post_ffn_norm),
                    tile=out_tile, sub_tile=out_sub_tile, col_chunk=col_chunk)


def kernel(x, pre_mix_norm, w_in, mu_shift, w0, w_decay_up, a0, w_aaa_up, w_gate_up, k_k, k_a, r_k, ln_x_w, ln_x_b,
           w_pool, pool_scale, w_out, post_mix_norm, pre_ffn_norm, w_up, conv_w, conv_b, w_down, post_ffn_norm):
    depth = w_in.shape[0]
    for layer in range(depth):
        x = _layer(x, pre_mix_norm[layer], w_in[layer], mu_shift[layer], w0[layer], w_decay_up[layer], a0[layer],
                   w_aaa_up[layer], w_gate_up[layer], k_k[layer], k_a[layer], r_k[layer], ln_x_w[layer],
                   ln_x_b[layer], w_pool[layer], pool_scale[layer], w_out[layer], post_mix_norm[layer],
                   pre_ffn_norm[layer], w_up[layer], conv_w[layer], conv_b[layer], w_down[layer],
                   post_ffn_norm[layer], in_tile=512, in_sub_tiles=(256, 256), rwkv_tile=256, rwkv_batch_block=4,
                   out_tile=1024, out_sub_tile=256, col_chunk=256)
    return x
```

```python
import functools
import math

import jax
import jax.numpy as jnp
from jax import lax
from jax.experimental import pallas as pl
from jax.experimental.pallas import tpu as pltpu

F32 = jnp.float32
BF16 = jnp.bfloat16

HEAD_DIM = 64
POOL_WINDOWS = (2, 4, 8, 16)
DECAY_LORA = 64
AAA_LORA = 64
GATE_LORA = 128
CONV_WIDTH = 3
NORM_EPS = 1e-6
GN_EPS = 64e-5
EXP_NEG_HALF = math.exp(-0.5)

CHUNK = 64
GROUP_HEADS = 4
GROUP_LANES = GROUP_HEADS * HEAD_DIM
POOL_HISTORY = 16
CONV_HISTORY = 8
VMEM_LIMIT_BYTES = 56 * 1024 * 1024


def _sigmoid(z):
    return 1.0 / (1.0 + jnp.exp(-z))


def _rms_norm(x, gain):
    ms = jnp.mean(x * x, axis=-1, keepdims=True)
    return x * lax.rsqrt(ms + NORM_EPS) * gain


def _dot(a, b, precision=None):
    return jnp.dot(a, b, preferred_element_type=F32, precision=precision)


def _dot_nt(a, b, precision=None):
    return lax.dot_general(a, b, (((1,), (1,)), ((), ())), preferred_element_type=F32, precision=precision)


def _dot_tn(a, b, precision=None):
    return lax.dot_general(a, b, (((0,), (0,)), ((), ())), preferred_element_type=F32, precision=precision)


def _split_dot(a, ones_bf16):
    hi = a.astype(BF16)
    lo = (a - hi.astype(F32)).astype(BF16)
    return _dot(hi, ones_bf16) + _dot(lo, ones_bf16)


def _mix_in_kernel(x_ref, g1_ref, w_in_ref, mu_ref, w0_ref, wd_ref, a0_ref, wa_ref, wg_ref, kk_ref, ka_ref,
                   wpool_ref, pscale_ref, ones_ref,
                   r_out, k_out, v_out, lw_out, a_out, b_out, g_out, pool_out,
                   ucarry_ref, pcarry_ref, proj_ref, *, tile, sub_tiles, rwkv_width, pool_group_dim):
    t = pl.program_id(1)

    @pl.when(t == 0)
    def _():
        ucarry_ref[...] = jnp.zeros_like(ucarry_ref)
        pcarry_ref[...] = jnp.zeros_like(pcarry_ref)

    starts = [sum(sub_tiles[:i]) for i in range(len(sub_tiles))]
    assert sum(sub_tiles) == tile
    for r0, n in zip(starts, sub_tiles):
        rows = slice(r0, r0 + n)
        h = _rms_norm(x_ref[0, rows, :], g1_ref[...]).astype(BF16)
        proj_ref[rows, :] = _dot(h, w_in_ref[...])

    outs = (r_out, k_out, v_out, lw_out, a_out, b_out, g_out, pool_out)
    params = (mu_ref, w0_ref, wd_ref, a0_ref, wa_ref, wg_ref, kk_ref, ka_ref, wpool_ref, pscale_ref, ones_ref)
    for r0, n in zip(starts, sub_tiles):
        rows = slice(r0, r0 + n)
        _mix_in_rows(proj_ref[rows, :], t * tile + r0, rows, params, outs, ucarry_ref, pcarry_ref,
                     rwkv_width=rwkv_width, pool_group_dim=pool_group_dim)


def _mix_in_rows(proj, pos0, rows, params, outs, ucarry_ref, pcarry_ref, *, rwkv_width, pool_group_dim):
    mu_ref, w0_ref, wd_ref, a0_ref, wa_ref, wg_ref, kk_ref, ka_ref, wpool_ref, pscale_ref, ones_ref = params
    r_out, k_out, v_out, lw_out, a_out, b_out, g_out, pool_out = outs
    n = proj.shape[0]
    rwkv_cols = 3 * rwkv_width + DECAY_LORA + AAA_LORA + GATE_LORA
    row = lax.broadcasted_iota(jnp.int32, (n, 1), 0)

    u = proj[:, :rwkv_cols]
    prev = pltpu.roll(u, 1, 0)
    prev = jnp.where(row == 0, ucarry_ref[...], prev)
    ucarry_ref[...] = u[n - 1:n, :]
    u = u + mu_ref[...] * (prev - u)

    w = rwkv_width
    r = u[:, 0:w]
    k = u[:, w:2 * w]
    v = u[:, 2 * w:3 * w]
    o = 3 * w
    wl = u[:, o:o + DECAY_LORA]
    al = u[:, o + DECAY_LORA:o + DECAY_LORA + AAA_LORA]
    gl = u[:, o + DECAY_LORA + AAA_LORA:rwkv_cols]

    z = w0_ref[...] + _dot(jnp.tanh(wl).astype(BF16), wd_ref[...])
    lw = -EXP_NEG_HALF * _sigmoid(z)
    alpha = _sigmoid(a0_ref[...] + _dot(al.astype(BF16), wa_ref[...]))
    g = _dot(_sigmoid(gl).astype(BF16), wg_ref[...])

    kk = k * kk_ref[...]
    kk2 = (kk * kk).astype(BF16)
    gl = GROUP_LANES
    ss = jnp.concatenate([_dot(kk2[:, i * gl:(i + 1) * gl], ones_ref[...]) for i in range(w // gl)], axis=1)
    kk = kk * lax.rsqrt(jnp.maximum(ss, 1e-24))
    k_mod = k * (1.0 + (alpha - 1.0) * ka_ref[...])

    r_out[0, rows, :] = r
    k_out[0, rows, :] = k_mod
    v_out[0, rows, :] = v
    lw_out[0, rows, :] = lw
    a_out[0, rows, :] = -kk
    b_out[0, rows, :] = kk * alpha
    g_out[0, rows, :] = g

    up = proj[:, rwkv_cols:]
    ext = jnp.concatenate([pcarry_ref[...], up], axis=0)
    pcarry_ref[...] = up[n - POOL_HISTORY:, :]
    pos = (pos0 + row + 1).astype(F32)
    gd = pool_group_dim
    pooled = []
    for gi, win in enumerate(POOL_WINDOWS):
        s = ext[:, gi * gd:(gi + 1) * gd]
        step = 1
        while step < win:
            s = s + pltpu.roll(s, step, 0)
            step *= 2
        cnt = jnp.minimum(pos, float(win))
        pooled.append((s[POOL_HISTORY:, :] / cnt - up[:, gi * gd:(gi + 1) * gd]).astype(BF16))
    for j in range(len(POOL_WINDOWS) // 2):
        mixed = _dot(jnp.concatenate(pooled[2 * j:2 * j + 2], axis=1), wpool_ref[j])
        pool_out[0, rows, 2 * j * gd:2 * (j + 1) * gd] = (
            mixed * pscale_ref[:, 2 * j * gd:2 * (j + 1) * gd]).astype(BF16)


def _mix_in(x, g1, w_in, mu, w0, wd, a0, wa, wg, k_k, k_a, w_pool, pool_scale, ones, *, tile, sub_tiles):
    bsz, t_len, d_model = x.shape
    rwkv_width = w0.shape[-1]
    pool_width = pool_scale.shape[-1]
    pool_group_dim = w_pool.shape[-1] // 2
    rwkv_cols = mu.shape[-1]
    grid = (bsz, t_len // tile)

    def full(a):
        nd = a.ndim
        return pl.BlockSpec(a.shape, lambda b, t, _nd=nd: (0,) * _nd)

    seq_spec = lambda width: pl.BlockSpec((1, tile, width), lambda b, t: (b, t, 0))
    out_sds = lambda width, dtype=F32: jax.ShapeDtypeStruct((bsz, t_len, width), dtype)
    params = (g1, w_in, mu, w0, wd, a0, wa, wg, k_k, k_a, w_pool, pool_scale, ones)
    kern = functools.partial(_mix_in_kernel, tile=tile, sub_tiles=sub_tiles, rwkv_width=rwkv_width,
                             pool_group_dim=pool_group_dim)
    return pl.pallas_call(
        kern,
        grid=grid,
        in_specs=[seq_spec(d_model)] + [full(p) for p in params],
        out_specs=[seq_spec(rwkv_width)] * 7 + [seq_spec(pool_width)],
        out_shape=[out_sds(rwkv_width)] * 7 + [out_sds(pool_width, BF16)],
        scratch_shapes=[pltpu.VMEM((1, rwkv_cols), F32), pltpu.VMEM((POOL_HISTORY, pool_width), F32),
                        pltpu.VMEM((tile, w_in.shape[-1]), F32)],
        compiler_params=pltpu.CompilerParams(dimension_semantics=("arbitrary", "arbitrary"),
                                             vmem_limit_bytes=VMEM_LIMIT_BYTES),
        name="mix_in",
    )(x, *params)


def _block_diag(x, block_mask):
    return jnp.where(block_mask, jnp.concatenate([x] * GROUP_HEADS, axis=0), jnp.zeros((), x.dtype))


def _split3_dot(m3_bf16, a):
    hi = a.astype(BF16)
    r1 = a - hi.astype(F32)
    mid = r1.astype(BF16)
    lo = (r1 - mid.astype(F32)).astype(BF16)
    return _dot(m3_bf16, jnp.concatenate([hi, mid, lo], axis=0))


def _rwkv_kernel(r_ref, k_ref, v_ref, lw_ref, a_ref, b_ref, g_ref, rk_ref, lnw_ref, lnb_ref, y_ref, s_ref, tr_ref,
                 *, batch_block, groups, chunks, chunk_skew):
    t = pl.program_id(1)

    @pl.when(t == 0)
    def _():
        s_ref[...] = jnp.zeros_like(s_ref)

    c_len, gl = CHUNK, GROUP_LANES
    row = lax.broadcasted_iota(jnp.int32, (c_len, gl), 0)
    lane_pos = lax.broadcasted_iota(jnp.int32, (c_len, gl), 1) % c_len
    strict = lane_pos < row
    incl = lane_pos <= row
    eye = (lane_pos == row).astype(F32)
    bm_r = lax.broadcasted_iota(jnp.int32, (gl, gl), 0) // HEAD_DIM
    bm_c = lax.broadcasted_iota(jnp.int32, (gl, gl), 1) // HEAD_DIM
    block_mask = bm_r == bm_c
    head_ones = block_mask.astype(BF16)
    tri = (lax.broadcasted_iota(jnp.int32, (c_len, c_len), 1)
           <= lax.broadcasted_iota(jnp.int32, (c_len, c_len), 0)).astype(BF16)
    tri3 = jnp.concatenate([tri, tri, tri], axis=1)
    bd = lambda m: _block_diag(m.astype(BF16), block_mask)
    bf = lambda m: m.astype(BF16)

    def chunk_stages(bi, gi, c):
        rows = slice(c * c_len, (c + 1) * c_len)
        lanes = slice(gi * gl, (gi + 1) * gl)
        si = bi * groups + gi
        lw = lw_ref[bi, rows, lanes]
        cl = _split3_dot(tri3, lw)
        yield

        cl_last = cl[c_len - 1:c_len, :]
        a = a_ref[bi, rows, lanes]
        b = b_ref[bi, rows, lanes]
        r = r_ref[bi, rows, lanes]
        k = k_ref[bi, rows, lanes]
        e_neg = jnp.exp(-cl)
        e_end = jnp.exp(cl_last - cl)
        bkh = bf(jnp.concatenate([b * e_end, k * e_end], axis=0))
        p_end = jnp.exp(cl_last)
        ar = bf(jnp.concatenate([a * jnp.exp(cl - lw), r * jnp.exp(cl)], axis=0))
        tr_ref[2 * si] = bd(b * e_neg).T
        tr_ref[2 * si + 1] = bd(k * e_neg).T
        ab = _dot(ar, tr_ref[2 * si])
        ak = _dot(ar, tr_ref[2 * si + 1])
        yield

        a_ab = jnp.where(strict, ab[:c_len], 0.0)
        a_rb = bf(jnp.where(incl, ab[c_len:], 0.0))
        a_akr = bf(jnp.concatenate([jnp.where(strict, ak[:c_len], 0.0), jnp.where(incl, ak[c_len:], 0.0)], axis=0))
        lp = _dot(bf(a_ab), bd(a_ab))
        tm = eye + a_ab
        yield
        n_sq = int(math.log2(c_len)) - 1
        for j in range(1, n_sq + 1):
            wmat = bd(lp)
            if j < n_sq:
                res = _dot(bf(jnp.concatenate([lp, tm], axis=0)), wmat)
                lp = res[:c_len]
                tm = tm + res[c_len:]
            else:
                tm = bf(tm + _dot(bf(tm), wmat))
            yield

        v = v_ref[bi, rows, lanes]
        s_bd = s_ref[si]
        sar = _dot_nt(ar, bf(s_bd))
        av = _dot(a_akr, bd(v))
        yield

        u = _dot(tm, bd(sar[:c_len] + av[:c_len]))
        yield

        y = sar[c_len:] + av[c_len:] + _dot(a_rb, bd(u))
        upd = _dot_tn(bf(jnp.concatenate([u, v], axis=0)), bkh)
        yield

        s_ref[si] = s_bd * p_end + jnp.where(block_mask, upd, 0.0)
        rk_prod = r_ref[bi, rows, lanes] * k_ref[bi, rows, lanes] * rk_ref[:, lanes]
        sums = yield jnp.concatenate([y, rk_prod], axis=0)
        mean = sums[:c_len] * (1.0 / HEAD_DIM)
        bonus = sums[c_len:] * v
        d = y - mean
        var = (yield d * d) * (1.0 / HEAD_DIM)
        yn = d * lax.rsqrt(var + GN_EPS) * lnw_ref[:, lanes] + lnb_ref[:, lanes]
        y_ref[bi, rows, lanes] = ((yn + bonus) * g_ref[bi, rows, lanes]).astype(BF16)

    def advance(streams, asked):
        if asked[0] is None:
            replies = [None] * len(streams)
        else:
            n_rows = asked[0].shape[0]
            sums = _dot(bf(jnp.concatenate(asked, axis=0)), head_ones)
            replies = [sums[i * n_rows:(i + 1) * n_rows] for i in range(len(streams))]
        alive, new_asked = [], []
        for s, reply in zip(streams, replies):
            try:
                new_asked.append(s.send(reply))
                alive.append(s)
            except StopIteration:
                pass
        return alive, new_asked

    in_flight = {}
    started, tick = 0, 0
    while started < chunks or in_flight:
        for c in sorted(in_flight):
            streams, asked = advance(*in_flight[c])
            if streams:
                in_flight[c] = (streams, asked)
            else:
                del in_flight[c]
        if started < chunks and tick >= started * chunk_skew:
            streams = [chunk_stages(bi, gi, started) for bi in range(batch_block) for gi in range(groups)]
            in_flight[started] = (streams, [next(s) for s in streams])
            started += 1
        tick += 1


def _rwkv(r, k, v, lw, a, b, g, r_k, ln_w, ln_b, *, tile, batch_block):
    bsz, t_len, width = r.shape
    groups = width // GROUP_LANES
    chunks = tile // CHUNK
    grid = (bsz // batch_block, t_len // tile)
    seq_spec = pl.BlockSpec((batch_block, tile, width), lambda bi, ti: (bi, ti, 0))
    par_spec = pl.BlockSpec((1, width), lambda bi, ti: (0, 0))
    return pl.pallas_call(
        functools.partial(_rwkv_kernel, batch_block=batch_block, groups=groups, chunks=chunks, chunk_skew=4),
        grid=grid,
        in_specs=[seq_spec] * 7 + [par_spec] * 3,
        out_specs=seq_spec,
        out_shape=jax.ShapeDtypeStruct((bsz, t_len, width), BF16),
        scratch_shapes=[pltpu.VMEM((batch_block * groups, GROUP_LANES, GROUP_LANES), F32),
                        pltpu.VMEM((2 * batch_block * groups, GROUP_LANES, GROUP_LANES), BF16)],
        compiler_params=pltpu.CompilerParams(dimension_semantics=("arbitrary", "arbitrary"),
                                             vmem_limit_bytes=VMEM_LIMIT_BYTES),
        name="rwkv",
    )(r, k, v, lw, a, b, g, r_k, ln_w, ln_b)


def _gelu_tanh(x):
    c1 = -2.0 * math.sqrt(2.0 / math.pi) * math.log2(math.e)
    c2 = c1 * 0.044715
    return x / (1.0 + jnp.exp2(x * (c1 + c2 * (x * x))))


def _mix_out_kernel(x_ref, yr_ref, yp_ref, w_out_ref, g2_ref, g3_ref, w_up_ref, cw_ref, cb_ref, w_down_ref, g4_ref,
                    out_ref, ucarry_ref, h2_ref, act_ref, *, tile, sub_tile, d_ff, col_chunk, down_at):
    t = pl.program_id(1)
    n_sub = tile // sub_tile
    n_chunks = d_ff // col_chunk

    @pl.when(t == 0)
    def _():
        ucarry_ref[...] = jnp.zeros_like(ucarry_ref)

    for i in range(n_sub):
        rows = slice(i * sub_tile, (i + 1) * sub_tile)
        ycat = jnp.concatenate([yr_ref[0, rows, :], yp_ref[0, rows, :]], axis=-1)
        x1 = x_ref[0, rows, :] + _rms_norm(_dot(ycat, w_out_ref[...]), g2_ref[...])
        out_ref[0, rows, :] = x1
        h2_ref[rows, :] = _rms_norm(x1, g3_ref[...]).astype(BF16)

    def up_proj(i, c):
        h2 = h2_ref[i * sub_tile:(i + 1) * sub_tile, :]
        return [_dot(h2, w_up_ref[:, c0:c0 + col_chunk]) for c0 in (c * col_chunk, d_ff + c * col_chunk)]

    def conv(up, c0):
        cols = slice(c0, c0 + col_chunk)
        ext = jnp.concatenate([ucarry_ref[:, cols], up], axis=0)
        ucarry_ref[:, cols] = up[sub_tile - CONV_HISTORY:, :]
        s1 = pltpu.roll(ext, 1, 0)[CONV_HISTORY:, :]
        s2 = pltpu.roll(ext, 2, 0)[CONV_HISTORY:, :]
        return (up * cw_ref[2:3, cols] + s1 * cw_ref[1:2, cols] + s2 * cw_ref[0:1, cols]) + cb_ref[:, cols]

    d_model = w_down_ref.shape[-1]
    n_out_blocks = len(down_at)
    out_block = d_model // n_out_blocks

    def act_rows(i):
        slot = i % 2
        return slice(slot * sub_tile, (slot + 1) * sub_tile)

    def down_block(i, j):
        return _dot(act_ref[act_rows(i), :], w_down_ref[:, j * out_block:(j + 1) * out_block])

    def finish(i, f_blocks):
        rows = slice(i * sub_tile, (i + 1) * sub_tile)
        f = jnp.concatenate(f_blocks, axis=1)
        out_ref[0, rows, :] = out_ref[0, rows, :] + _rms_norm(f, g4_ref[...])

    for i in range(n_sub):
        rows = act_rows(i)
        f_blocks = []
        for c in range(n_chunks):
            gate_up, val_up = up_proj(i, c)
            if i > 0 and c in down_at:
                f_blocks.append(down_block(i - 1, down_at.index(c)))
                if len(f_blocks) == n_out_blocks:
                    finish(i - 1, f_blocks)
            c0 = c * col_chunk
            act = _gelu_tanh(conv(gate_up, c0)) * conv(val_up, d_ff + c0)
            act_ref[rows, c0:c0 + col_chunk] = act.astype(BF16)
    finish(n_sub - 1, [down_block(n_sub - 1, j) for j in range(n_out_blocks)])


def _mix_out(x, y_rwkv, y_pool, w_out, g2, g3, w_up, conv_w, conv_b, w_down, g4, *, tile, sub_tile, col_chunk):
    bsz, t_len, d_model = x.shape
    d_ff = w_down.shape[0]
    grid = (bsz, t_len // tile)

    def full(a):
        nd = a.ndim
        return pl.BlockSpec(a.shape, lambda b, t, _nd=nd: (0,) * _nd, pipeline_mode=pl.Buffered(1))

    seq_spec = lambda width: pl.BlockSpec((1, tile, width), lambda b, t: (b, t, 0))
    params = (w_out, g2, g3, w_up, conv_w, conv_b, w_down, g4)
    kern = functools.partial(_mix_out_kernel, tile=tile, sub_tile=sub_tile, d_ff=d_ff, col_chunk=col_chunk,
                             down_at=(4, 6, 8, 10))
    return pl.pallas_call(
        kern,
        grid=grid,
        in_specs=[seq_spec(d_model), seq_spec(y_rwkv.shape[-1]), seq_spec(y_pool.shape[-1])]
                 + [full(p) for p in params],
        out_specs=seq_spec(d_model),
        out_shape=jax.ShapeDtypeStruct((bsz, t_len, d_model), F32),
        scratch_shapes=[pltpu.VMEM((CONV_HISTORY, 2 * d_ff), F32), pltpu.VMEM((tile, d_model), BF16),
                        pltpu.VMEM((2 * sub_tile, d_ff), BF16)],
        compiler_params=pltpu.CompilerParams(dimension_semantics=("arbitrary", "arbitrary"),
                                             vmem_limit_bytes=VMEM_LIMIT_BYTES),
        name="mix_out",
    )(x, y_rwkv, y_pool, *params)


def _layer(x, pre_mix_norm, w_in, mu_shift, w0, w_decay_up, a0, w_aaa_up, w_gate_up, k_k, k_a, r_k, ln_x_w, ln_x_b,
           w_pool, pool_scale, w_out, post_mix_norm, pre_ffn_norm, w_up, conv_w, conv_b, w_down, post_ffn_norm,
           *, in_tile, in_sub_tiles, rwkv_tile, rwkv_batch_block, out_tile, out_sub_tile, col_chunk):
    row = lambda p: p.reshape(1, -1).astype(F32)
    head_id = jnp.arange(GROUP_LANES) // HEAD_DIM
    ones = (head_id[:, None] == head_id[None, :]).astype(BF16)
    n_groups, gd, _ = w_pool.shape
    wp = w_pool.astype(BF16).reshape(n_groups // 2, 2, gd, gd)
    zero = jnp.zeros_like(wp[:, 0])
    w_pool_pairs = jnp.concatenate([jnp.concatenate([wp[:, 0], zero], axis=2),
                                    jnp.concatenate([zero, wp[:, 1]], axis=2)], axis=1)

    r, k, v, lw, a, b, g, y_pool = _mix_in(
        x, row(pre_mix_norm), w_in.astype(BF16), row(mu_shift), row(w0), w_decay_up.astype(BF16), row(a0),
        w_aaa_up.astype(BF16), w_gate_up.astype(BF16), row(k_k), row(k_a), w_pool_pairs, row(pool_scale),
        ones, tile=in_tile, sub_tiles=in_sub_tiles)
    y_rwkv = _rwkv(r, k, v, lw, a, b, g, row(r_k), row(ln_x_w), row(ln_x_b), tile=rwkv_tile,
                   batch_block=rwkv_batch_block)
    return _mix_out(x, y_rwkv, y_pool, w_out.astype(BF16), row(post_mix_norm), row(pre_ffn_norm),
                    w_up.astype(BF16), conv_w.astype(F32), row(conv_b), w_down.astype(BF16), row(post_ffn_norm),
                    tile=out_tile, sub_tile=out_sub_tile, col_chunk=col_chunk)


def kernel(x, pre_mix_norm, w_in, mu_shift, w0, w_decay_up, a0, w_aaa_up, w_gate_up, k_k, k_a, r_k, ln_x_w, ln_x_b,
           w_pool, pool_scale, w_out, post_mix_norm, pre_ffn_norm, w_up, conv_w, conv_b, w_down, post_ffn_norm):
    depth = w_in.shape[0]
    for layer in range(depth):
        x = _layer(x, pre_mix_norm[layer], w_in[layer], mu_shift[layer], w0[layer], w_decay_up[layer], a0[layer],
                   w_aaa_up[layer], w_gate_up[layer], k_k[layer], k_a[layer], r_k[layer], ln_x_w[layer],
                   ln_x_b[layer], w_pool[layer], pool_scale[layer], w_out[layer], post_mix_norm[layer],
                   pre_ffn_norm[layer], w_up[layer], conv_w[layer], conv_b[layer], w_down[layer],
                   post_ffn_norm[layer], in_tile=512, in_sub_tiles=(256, 256), rwkv_tile=256, rwkv_batch_block=4,
                   out_tile=1024, out_sub_tile=256, col_chunk=256)
    return x
```
